```python
import jax, jax.numpy as jnp
from jax import lax
import numpy as np

D_MODEL = 2048
BATCH = 4
SEQ = 2048
DEPTH = 4
DEC_BATCH = 8
DEC_SEQ = 8
PAST_LEN = 16384
PAGE_SIZE = 128

N_MIXERS = 2
N_ATTN_LAYERS = (DEPTH + 1) // 2
N_POOL_LAYERS = DEPTH // 2
N_HEADS = 16
HEAD_DIM = 128
ATTN_W = N_HEADS * HEAD_DIM
N_IDX_HEADS = 16
IDX_DIM = 128
TOPK_MAX = 256
Q_BLOCK = 32
ATTN_SCALE = HEAD_DIM ** -0.5
IDX_SCALE = IDX_DIM ** -0.5
OFF_K = ATTN_W
OFF_V = 2 * ATTN_W
OFF_QI = 3 * ATTN_W
OFF_KI = OFF_QI + N_IDX_HEADS * IDX_DIM
OFF_WI = OFF_KI + IDX_DIM
IN_W = OFF_WI + N_IDX_HEADS
POOL_WINDOWS = (2, 4, 8, 16)
N_POOL_GROUPS = len(POOL_WINDOWS)
POOL_GROUP = D_MODEL // N_POOL_GROUPS
POOL_HIST = max(POOL_WINDOWS) - 1
N_EXPERTS = 32
TOP_K = 4
D_FF = D_MODEL
SWIGLU_LIMIT = 7.0
SWIGLU_ALPHA = 1.702
LN_EPS = 1e-5
DEEPNORM_ALPHA = (2 * DEPTH) ** 0.25
DEEPNORM_BETA = (8 * DEPTH) ** -0.25

kernel_name = 'dsa_pool_moe_deepnorm_step'


def layer_norm(x, g, b):
    xf = x.astype(jnp.float32)
    mu = jnp.mean(xf, axis=-1, keepdims=True)
    var = jnp.mean(jnp.square(xf - mu), axis=-1, keepdims=True)
    return ((xf - mu) * lax.rsqrt(var + LN_EPS) * g.astype(jnp.float32) + b.astype(jnp.float32)).astype(x.dtype)


def post_norm(x, h, g, b):
    return layer_norm(DEEPNORM_ALPHA * x + h, g, b)


def attn_project(x, w_in, kn_g, kn_b):
    B, T, _ = x.shape
    p = x @ w_in
    q = p[..., :OFF_K].reshape(B, T, N_HEADS, HEAD_DIM)
    k = p[..., OFF_K:OFF_V].reshape(B, T, N_HEADS, HEAD_DIM)
    v = p[..., OFF_V:OFF_QI].reshape(B, T, N_HEADS, HEAD_DIM)
    qi = p[..., OFF_QI:OFF_KI].reshape(B, T, N_IDX_HEADS, IDX_DIM)
    ki = layer_norm(p[..., OFF_KI:OFF_WI], kn_g, kn_b)
    wi = p[..., OFF_WI:] * (N_IDX_HEADS ** -0.5)
    return q, k, v, qi, ki, wi


def indexer_select(qi, wi, ki, q_pos, n_sel):
    S = ki.shape[1]
    dots = jnp.einsum('bqhd,bsd->bqhs', qi, ki, preferred_element_type=jnp.float32)
    score = jnp.einsum('bqhs,bqh->bqs', jax.nn.relu(dots), wi.astype(jnp.float32)) * IDX_SCALE
    admissible = jnp.arange(S)[None, None, :] <= q_pos[None, :, None]
    score = jnp.where(admissible, score, -jnp.inf)
    _, idx = lax.top_k(score, n_sel)
    return idx, idx <= q_pos[None, :, None]


def sparse_attend(q, k_sel, v_sel, valid):
    logits = jnp.einsum('bqhd,bqnhd->bqhn', q, k_sel, preferred_element_type=jnp.float32) * ATTN_SCALE
    logits = jnp.where(valid[:, :, None, :], logits, -jnp.inf)
    p = jax.nn.softmax(logits, axis=-1)
    o = jnp.einsum('bqhn,bqnhd->bqhd', p.astype(v_sel.dtype), v_sel, preferred_element_type=jnp.float32)
    return o.astype(q.dtype)


def dsa_prompt(q, k, v, qi, ki, wi):
    B, T = q.shape[:2]
    n_sel = min(TOPK_MAX, T // 4)
    qb = Q_BLOCK if T % Q_BLOCK == 0 else T
    nb = T // qb

    def blocks(a):
        return jnp.moveaxis(a.reshape((B, nb, qb) + a.shape[2:]), 1, 0)

    gather = jax.vmap(lambda rows, ix: rows[ix])

    def one_block(args):
        q_b, qi_b, wi_b, pos_b = args
        idx, valid = indexer_select(qi_b, wi_b, ki, pos_b, n_sel)
        return sparse_attend(q_b, gather(k, idx), gather(v, idx), valid)

    o = lax.map(one_block, (blocks(q), blocks(qi), blocks(wi), jnp.arange(T).reshape(nb, qb)))
    return jnp.moveaxis(o, 0, 1).reshape(B, T, ATTN_W)


def dsa_sample(q, k, v, qi, ki, wi, ck, cv, cki, page_table):
    B, Tn = q.shape[:2]
    n_pages = page_table.shape[1]
    past = n_pages * PAGE_SIZE
    n_sel = min(TOPK_MAX, (past + Tn) // 4)
    ki_all = jnp.concatenate([cki[page_table].reshape(B, past, IDX_DIM), ki.astype(cki.dtype)], axis=1)
    q_pos = past + jnp.arange(Tn)
    idx, valid = indexer_select(qi, wi, ki_all, q_pos, n_sel)
    b_ix = jnp.arange(B)[:, None, None]
    ip = jnp.minimum(idx, past - 1)
    phys = page_table[b_ix, ip // PAGE_SIZE]
    slot = ip % PAGE_SIZE
    inew = jnp.clip(idx - past, 0, Tn - 1)
    in_past = (idx < past)[..., None, None]
    k_sel = jnp.where(in_past, ck[phys, slot], k[b_ix, inew])
    v_sel = jnp.where(in_past, cv[phys, slot], v[b_ix, inew])
    return sparse_attend(q, k_sel, v_sel, valid).reshape(B, Tn, ATTN_W)


def pool_mix(x, hist, start_pos, w_pool, scale):
    B, T, D = x.shape
    xe = jnp.concatenate([hist.astype(x.dtype), x], axis=1)
    xf = xe.astype(jnp.float32)
    cs = jnp.concatenate([jnp.zeros((B, 1, D), jnp.float32), jnp.cumsum(xf, axis=1)], axis=1)
    pos = start_pos + jnp.arange(T)
    groups = []
    for g, w in enumerate(POOL_WINDOWS):
        c0, c1 = g * POOL_GROUP, (g + 1) * POOL_GROUP
        win_sum = cs[:, POOL_HIST + 1:, c0:c1] - cs[:, POOL_HIST + 1 - w:POOL_HIST + 1 - w + T, c0:c1]
        count = jnp.minimum(w, pos + 1).astype(jnp.float32)[None, :, None]
        groups.append(win_sum / count - xf[:, POOL_HIST:, c0:c1])
    mixed = jnp.stack(groups, axis=2).astype(x.dtype)
    y = jnp.einsum('btgc,gcd->btgd', mixed, w_pool).reshape(B, T, D) * scale
    return y, xe[:, -POOL_HIST:]


def moe(x, w_r, b_r, w_gu, b_gu, w_d, b_d):
    B, T, D = x.shape
    xt = x.reshape(B * T, D)
    logits = (xt @ w_r + b_r).astype(jnp.float32)
    top_v, top_i = lax.top_k(logits, TOP_K)
    gk = jax.nn.softmax(top_v, axis=-1)
    gates = jnp.sum(jax.nn.one_hot(top_i, N_EXPERTS, dtype=jnp.float32) * gk[..., None], axis=1)
    out = jnp.zeros((B * T, D), jnp.float32)
    for e in range(N_EXPERTS):
        h = xt @ w_gu[e] + b_gu[e]
        gate = jnp.minimum(h[:, :D_FF], SWIGLU_LIMIT)
        lin = jnp.clip(h[:, D_FF:], -SWIGLU_LIMIT, SWIGLU_LIMIT)
        act = gate * jax.nn.sigmoid(SWIGLU_ALPHA * gate) * (lin + 1)
        out = out + gates[:, e:e + 1] * (act @ w_d[e] + b_d[e])
    return out.astype(x.dtype).reshape(B, T, D)


def setup_inputs(seed: int = 0) -> dict:
    key = jax.random.key(seed)
    ks = jax.random.split(key, 24)
    f32 = jnp.float32
    n_pages = PAST_LEN // PAGE_SIZE
    n_pool_pages = (DEC_BATCH * n_pages * 5) // 4

    def nrm(k, shape, s=1.0):
        return jax.random.normal(k, shape, f32) * s

    page_table = jax.random.permutation(ks[6], n_pool_pages)[:DEC_BATCH * n_pages]
    page_table = page_table.reshape(DEC_BATCH, n_pages).astype(jnp.int32)
    col_scale = jnp.ones((IN_W,), f32).at[OFF_V:OFF_QI].set(DEEPNORM_BETA)
    return {
        'x_prompt': nrm(ks[0], (BATCH, SEQ, D_MODEL)),
        'x_sample': nrm(ks[1], (DEC_BATCH, DEC_SEQ, D_MODEL)),
        'cache_k': nrm(ks[2], (N_ATTN_LAYERS, n_pool_pages, PAGE_SIZE, N_HEADS, HEAD_DIM)),
        'cache_v': nrm(ks[3], (N_ATTN_LAYERS, n_pool_pages, PAGE_SIZE, N_HEADS, HEAD_DIM), DEEPNORM_BETA),
        'cache_kidx': nrm(ks[4], (N_ATTN_LAYERS, n_pool_pages, PAGE_SIZE, IDX_DIM)),
        'state_pool': nrm(ks[5], (N_POOL_LAYERS, DEC_BATCH, POOL_HIST, D_MODEL)),
        'page_table': page_table,
        'w_attn_in': nrm(ks[7], (N_ATTN_LAYERS, D_MODEL, IN_W), D_MODEL ** -0.5) * col_scale,
        'idx_knorm_g': 1.0 + nrm(ks[8], (N_ATTN_LAYERS, IDX_DIM), 0.02),
        'idx_knorm_b': nrm(ks[9], (N_ATTN_LAYERS, IDX_DIM), 0.02),
        'w_attn_out': nrm(ks[10], (N_ATTN_LAYERS, ATTN_W, D_MODEL), ATTN_W ** -0.5 * DEEPNORM_BETA),
        'w_pool': nrm(ks[11], (N_POOL_LAYERS, N_POOL_GROUPS, POOL_GROUP, POOL_GROUP), POOL_GROUP ** -0.5 * DEEPNORM_BETA),
        'pool_scale': 1.0 + nrm(ks[12], (N_POOL_LAYERS, D_MODEL), 0.1),
        'ln_g': 1.0 + nrm(ks[13], (DEPTH, 2, D_MODEL), 0.02),
        'ln_b': nrm(ks[14], (DEPTH, 2, D_MODEL), 0.02),
        'w_router': nrm(ks[15], (DEPTH, D_MODEL, N_EXPERTS), D_MODEL ** -0.5),
        'b_router': nrm(ks[16], (DEPTH, N_EXPERTS), 0.01),
        'w_gate_up': nrm(ks[17], (DEPTH, N_EXPERTS, D_MODEL, 2 * D_FF), D_MODEL ** -0.5 * DEEPNORM_BETA),
        'b_gate_up': nrm(ks[18], (DEPTH, N_EXPERTS, 2 * D_FF), 0.01),
        'w_down': nrm(ks[19], (DEPTH, N_EXPERTS, D_FF, D_MODEL), D_FF ** -0.5 * DEEPNORM_BETA),
        'b_down': nrm(ks[20], (DEPTH, N_EXPERTS, D_MODEL), 0.01),
    }


def reference(x_prompt, x_sample, cache_k, cache_v, cache_kidx, state_pool, page_table,
              w_attn_in, idx_knorm_g, idx_knorm_b, w_attn_out, w_pool, pool_scale,
              ln_g, ln_b, w_router, b_router, w_gate_up, b_gate_up, w_down, b_down):
    xp, xs = x_prompt, x_sample
    kp_rows, vp_rows, kip_rows, pool_p = [], [], [], []
    ks_rows, vs_rows, kis_rows, pool_s = [], [], [], []
    for i in range(DEPTH):
        j = i // N_MIXERS
        if i % N_MIXERS == 0:
            qp, kp, vp, qip, kip, wip = attn_project(xp, w_attn_in[j], idx_knorm_g[j], idx_knorm_b[j])
            hp = dsa_prompt(qp, kp, vp, qip, kip, wip) @ w_attn_out[j]
            qs, ksn, vsn, qis, kis, wis = attn_project(xs, w_attn_in[j], idx_knorm_g[j], idx_knorm_b[j])
            hs = dsa_sample(qs, ksn, vsn, qis, kis, wis, cache_k[j], cache_v[j], cache_kidx[j], page_table) @ w_attn_out[j]
            kp_rows.append(kp)
            vp_rows.append(vp)
            kip_rows.append(kip)
            ks_rows.append(ksn)
            vs_rows.append(vsn)
            kis_rows.append(kis)
        else:
            zero_hist = jnp.zeros((xp.shape[0], POOL_HIST, D_MODEL), xp.dtype)
            hp, sp = pool_mix(xp, zero_hist, 0, w_pool[j], pool_scale[j])
            hs, ss = pool_mix(xs, state_pool[j], PAST_LEN, w_pool[j], pool_scale[j])
            pool_p.append(sp)
            pool_s.append(ss)
        xp = post_norm(xp, hp, ln_g[i, 0], ln_b[i, 0])
        xs = post_norm(xs, hs, ln_g[i, 0], ln_b[i, 0])
        xp = post_norm(xp, moe(xp, w_router[i], b_router[i], w_gate_up[i], b_gate_up[i], w_down[i], b_down[i]), ln_g[i, 1], ln_b[i, 1])
        xs = post_norm(xs, moe(xs, w_router[i], b_router[i], w_gate_up[i], b_gate_up[i], w_down[i], b_down[i]), ln_g[i, 1], ln_b[i, 1])
    return (xp, xs,
            jnp.stack(kp_rows), jnp.stack(vp_rows), jnp.stack(kip_rows), jnp.stack(pool_p),
            jnp.stack(ks_rows), jnp.stack(vs_rows), jnp.stack(kis_rows), jnp.stack(pool_s))
```

```python
import functools

import jax
import jax.numpy as jnp
from jax import lax
from jax.experimental import pallas as pl
from jax.experimental.pallas import tpu as pltpu

D_MODEL = 2048
BATCH = 4
SEQ = 2048
DEPTH = 4
DEC_BATCH = 8
DEC_SEQ = 8
PAST_LEN = 16384
PAGE_SIZE = 128

N_MIXERS = 2
N_HEADS = 16
HEAD_DIM = 128
ATTN_W = N_HEADS * HEAD_DIM
N_IDX_HEADS = 16
IDX_DIM = 128
IDX_W = N_IDX_HEADS * IDX_DIM
TOPK_MAX = 256
ATTN_SCALE = HEAD_DIM ** -0.5
IDX_SCALE = IDX_DIM ** -0.5
OFF_K = ATTN_W
OFF_V = 2 * ATTN_W
OFF_QI = 3 * ATTN_W
OFF_KI = OFF_QI + IDX_W
OFF_WI = OFF_KI + IDX_DIM
IN_W = OFF_WI + N_IDX_HEADS
POOL_WINDOWS = (2, 4, 8, 16)
N_POOL_GROUPS = len(POOL_WINDOWS)
POOL_GROUP = D_MODEL // N_POOL_GROUPS
POOL_HIST = max(POOL_WINDOWS) - 1
N_EXPERTS = 32
TOP_K = 4
D_FF = D_MODEL
SWIGLU_LIMIT = 7.0
SWIGLU_ALPHA = 1.702
LN_EPS = 1e-5
DEEPNORM_ALPHA = (2 * DEPTH) ** 0.25

N_PROMPT = BATCH * SEQ
N_SAMPLE = DEC_BATCH * DEC_SEQ
N_TOK = N_PROMPT + N_SAMPLE
N_PAGES = PAST_LEN // PAGE_SIZE

LANES = 128
HALO = 16
VMEM_LIMIT = 56 * 1024 * 1024

TM_PROMPT = min(256, N_PROMPT)
TM_SAMPLE = N_SAMPLE
TM_MM = min(1024, N_PROMPT)
TN_MM = min(512, ATTN_W)
TQ = min(256, SEQ)
MOE_TM = min(256, N_PROMPT)
MOE_RT = 5
MOE_F = min(256, D_FF)
TOP_K_SHIFT = TOP_K.bit_length() - 1
N_ASSIGN = N_TOK * TOP_K
MOE_P = (N_ASSIGN // MOE_TM + N_EXPERTS) * MOE_TM
MOE_G = (N_ASSIGN // MOE_TM + N_EXPERTS) // MOE_RT + N_EXPERTS + 1

F32 = jnp.float32
BF16 = jnp.bfloat16
I32 = jnp.int32
NEG_INF = float("-inf")
INT_MIN = -2 ** 31

assert 1 << TOP_K_SHIFT == TOP_K


def _cparams(sem):
    return pltpu.CompilerParams(dimension_semantics=sem, vmem_limit_bytes=VMEM_LIMIT)


def _layer_norm(x, g, b):
    mu = jnp.mean(x, axis=-1, keepdims=True)
    xc = x - mu
    var = jnp.mean(xc * xc, axis=-1, keepdims=True)
    return xc * lax.rsqrt(var + LN_EPS) * g + b


def _post_norm_store(x, h, g_ref, b_ref, o_ref):
    o_ref[...] = _layer_norm(DEEPNORM_ALPHA * x + h, g_ref[...], b_ref[...])


def _mm_kernel(x_ref, w_ref, *o_refs):
    acc = jnp.dot(x_ref[...].astype(BF16), w_ref[...], preferred_element_type=F32)
    for o in o_refs:
        o[...] = acc.astype(o.dtype)


def _mm(x, row0, nrows, tm, w, col0, ncols, out_dtypes):
    k = x.shape[1]
    tn = min(TN_MM, ncols)
    rb0, cb0 = row0 // tm, col0 // tn
    return pl.pallas_call(
        _mm_kernel,
        grid=(nrows // tm, ncols // tn),
        in_specs=[pl.BlockSpec((tm, k), lambda i, j: (rb0 + i, 0)),
                  pl.BlockSpec((k, tn), lambda i, j: (0, cb0 + j))],
        out_specs=[pl.BlockSpec((tm, tn), lambda i, j: (i, j)) for _ in out_dtypes],
        out_shape=[jax.ShapeDtypeStruct((nrows, ncols), dt) for dt in out_dtypes],
        compiler_params=_cparams(("parallel", "arbitrary")),
    )(x, w)


def _kiwi_kernel(x_ref, w_ref, g_ref, b_ref, ki_ref, wi_ref):
    p = jnp.dot(x_ref[...].astype(BF16), w_ref[...], preferred_element_type=F32)
    ki_ref[...] = _layer_norm(p[:, :IDX_DIM], g_ref[...], b_ref[...])
    wi_ref[...] = p[:, IDX_DIM:IDX_DIM + N_IDX_HEADS] * (N_IDX_HEADS ** -0.5)


def _kiwi(x, row0, nrows, tm, w_kiwi, g, b):
    k = x.shape[1]
    rb0 = row0 // tm
    wcols = w_kiwi.shape[1]
    return pl.pallas_call(
        _kiwi_kernel,
        grid=(nrows // tm,),
        in_specs=[pl.BlockSpec((tm, k), lambda i: (rb0 + i, 0)),
                  pl.BlockSpec((k, wcols), lambda i: (0, 0)),
                  pl.BlockSpec((1, IDX_DIM), lambda i: (0, 0)),
                  pl.BlockSpec((1, IDX_DIM), lambda i: (0, 0))],
        out_specs=[pl.BlockSpec((tm, IDX_DIM), lambda i: (i, 0)),
                   pl.BlockSpec((tm, N_IDX_HEADS), lambda i: (i, 0))],
        out_shape=[jax.ShapeDtypeStruct((nrows, IDX_DIM), F32),
                   jax.ShapeDtypeStruct((nrows, N_IDX_HEADS), F32)],
        compiler_params=_cparams(("parallel",)),
    )(x, w_kiwi, g, b)


def _count(mask):
    return jnp.sum(jnp.where(mask, 1.0, 0.0), axis=-1, keepdims=True)


def _kth_largest_key(key, k):
    lo = jnp.where(_count(key >= 0) >= k, 0, INT_MIN).astype(I32)

    def body(i, lo):
        cand = lo + jnp.left_shift(jnp.int32(1), 30 - i)
        return jnp.where(_count(key >= cand) >= k, cand, lo)

    return lax.fori_loop(0, 31, body, lo)


def _select_bias(score, adm, k, key_ref, bias_ref):
    score = jnp.where(score == 0.0, 0.0, score)
    bits = pltpu.bitcast(score, I32)
    key = jnp.where(bits < 0, bits ^ jnp.int32(0x7FFFFFFF), bits)
    key = jnp.where(adm, key, INT_MIN)
    thr = jnp.maximum(_kth_largest_key(key, k), INT_MIN + 1)
    ge = key >= thr
    bias_ref[...] = jnp.where(ge, 0.0, NEG_INF)
    n_ge = _count(ge)

    @pl.when(jnp.max(n_ge) > k)
    def _ties():
        key_ref[...] = key
        need = k - _count(key > thr)
        before = (lax.broadcasted_iota(I32, (LANES, LANES), 0)
                  < lax.broadcasted_iota(I32, (LANES, LANES), 1))
        tri = jnp.where(before, 1.0, 0.0).astype(BF16)

        def tile(j, carry):
            off = pl.multiple_of(j * LANES, LANES)
            kj = key_ref[:, pl.ds(off, LANES)]
            eq = kj == thr
            e = jnp.where(eq, 1.0, 0.0)
            rank = jnp.dot(e.astype(BF16), tri, preferred_element_type=F32) + carry
            sel = (kj > thr) | (eq & (rank < need))
            bias_ref[:, pl.ds(off, LANES)] = jnp.where(sel, 0.0, NEG_INF)
            return carry + jnp.sum(e, axis=-1, keepdims=True)

        lax.fori_loop(0, key_ref.shape[1] // LANES, tile, jnp.zeros_like(need))


def _dsa_prompt_kernel(q_ref, qi_ref, wi_ref, ki_ref, k_ref, v_ref, o_ref, key_ref, bias_ref, *, n_sel):
    i = pl.program_id(1)
    s_len = ki_ref.shape[1]
    ki = ki_ref[0].astype(BF16)
    wi = wi_ref[0]
    score = jnp.zeros((TQ, s_len), F32)
    for h in range(N_IDX_HEADS):
        d = lax.dot_general(qi_ref[0, :, h * IDX_DIM:(h + 1) * IDX_DIM], ki,
                            (((1,), (1,)), ((), ())), preferred_element_type=F32)
        score = score + jnp.maximum(d, 0.0) * wi[:, h:h + 1]
    score = score * IDX_SCALE
    t_pos = i * TQ + lax.broadcasted_iota(I32, (TQ, s_len), 0)
    adm = lax.broadcasted_iota(I32, (TQ, s_len), 1) <= t_pos
    _select_bias(jnp.where(adm, score, NEG_INF), adm, n_sel, key_ref, bias_ref)

    for h in range(N_HEADS):
        cols = slice(h * HEAD_DIM, (h + 1) * HEAD_DIM)
        logits = lax.dot_general(q_ref[0, :, cols], k_ref[0, :, cols],
                                 (((1,), (1,)), ((), ())), preferred_element_type=F32)
        logits = logits * ATTN_SCALE + bias_ref[...]
        m = jnp.max(logits, axis=-1, keepdims=True)
        p = jnp.exp(logits - m)
        denom = jnp.sum(p, axis=-1, keepdims=True)
        o = jnp.dot(p.astype(BF16), v_ref[0, :, cols], preferred_element_type=F32)
        o_ref[0, :, cols] = (o / denom).astype(o_ref.dtype)


def _dsa_prompt(q, qi, wi, ki, k, v):
    b, t = q.shape[:2]
    n_sel = min(TOPK_MAX, t // 4)
    kern = functools.partial(_dsa_prompt_kernel, n_sel=n_sel)
    once = pl.Buffered(1)
    return pl.pallas_call(
        kern,
        grid=(b, t // TQ),
        in_specs=[pl.BlockSpec((1, TQ, ATTN_W), lambda bb, i: (bb, i, 0)),
                  pl.BlockSpec((1, TQ, IDX_W), lambda bb, i: (bb, i, 0)),
                  pl.BlockSpec((1, TQ, N_IDX_HEADS), lambda bb, i: (bb, i, 0)),
                  pl.BlockSpec((1, t, IDX_DIM), lambda bb, i: (bb, 0, 0), pipeline_mode=once),
                  pl.BlockSpec((1, t, ATTN_W), lambda bb, i: (bb, 0, 0), pipeline_mode=once),
                  pl.BlockSpec((1, t, ATTN_W), lambda bb, i: (bb, 0, 0), pipeline_mode=once)],
        out_specs=pl.BlockSpec((1, TQ, ATTN_W), lambda bb, i: (bb, i, 0)),
        out_shape=jax.ShapeDtypeStruct((b, t, ATTN_W), BF16),
        scratch_shapes=[pltpu.VMEM((TQ, t), I32), pltpu.VMEM((TQ, t), F32)],
        compiler_params=_cparams(("parallel", "arbitrary")),
    )(q, qi, wi, ki, k, v)


def _dsa_sample_select_kernel(pt_ref, qi_ref, wi_ref, cki_ref, kin_ref, bias_ref, score_ref, key_ref, *, n_sel):
    p = pl.program_id(1)
    tn = DEC_SEQ
    is_new = p == N_PAGES
    kpage = jnp.where(is_new, kin_ref[0], cki_ref[0]).astype(BF16)
    d = lax.dot_general(qi_ref[0], kpage, (((1,), (1,)), ((), ())), preferred_element_type=F32)
    r = jnp.maximum(d, 0.0) * wi_ref[0]
    score = jnp.sum(r.reshape(tn, N_IDX_HEADS, PAGE_SIZE), axis=1) * IDX_SCALE
    off = pl.multiple_of(p * PAGE_SIZE, PAGE_SIZE)
    score_ref[:, pl.ds(off, PAGE_SIZE)] = score

    @pl.when(is_new)
    def _select():
        s_len = score_ref.shape[1]
        q_pos = PAST_LEN + lax.broadcasted_iota(I32, (tn, s_len), 0)
        adm = lax.broadcasted_iota(I32, (tn, s_len), 1) <= q_pos
        sc = jnp.where(adm, score_ref[...], NEG_INF)
        _select_bias(sc, adm, n_sel, key_ref, bias_ref.at[0])


def _dsa_sample_select(page_table, qi_flat, wi_flat, cki, ki_new_pad):
    s_len = PAST_LEN + PAGE_SIZE
    n_sel = min(TOPK_MAX, (PAST_LEN + DEC_SEQ) // 4)
    rows = DEC_SEQ * N_IDX_HEADS
    kern = functools.partial(_dsa_sample_select_kernel, n_sel=n_sel)
    grid_spec = pltpu.PrefetchScalarGridSpec(
        num_scalar_prefetch=1,
        grid=(DEC_BATCH, N_PAGES + 1),
        in_specs=[pl.BlockSpec((1, rows, IDX_DIM), lambda b, p, pt: (b, 0, 0)),
                  pl.BlockSpec((1, rows, 1), lambda b, p, pt: (b, 0, 0)),
                  pl.BlockSpec((1, PAGE_SIZE, IDX_DIM),
                               lambda b, p, pt: (pt[b * N_PAGES + jnp.minimum(p, N_PAGES - 1)], 0, 0)),
                  pl.BlockSpec((1, PAGE_SIZE, IDX_DIM), lambda b, p, pt: (b, 0, 0))],
        out_specs=pl.BlockSpec((1, DEC_SEQ, s_len), lambda b, p, pt: (b, 0, 0)),
        scratch_shapes=[pltpu.VMEM((DEC_SEQ, s_len), F32), pltpu.VMEM((DEC_SEQ, s_len), I32)],
    )
    return pl.pallas_call(
        kern,
        grid_spec=grid_spec,
        out_shape=jax.ShapeDtypeStruct((DEC_BATCH, DEC_SEQ, s_len), F32),
        compiler_params=_cparams(("parallel", "arbitrary")),
    )(page_table.reshape(-1), qi_flat, wi_flat, cki, ki_new_pad)


def _dsa_sample_attend_kernel(pt_ref, q_ref, bias_ref, ck_ref, cv_ref, kn_ref, vn_ref, o_ref,
                              qb_ref, m_ref, l_ref, acc_ref):
    p = pl.program_id(1)
    tn = DEC_SEQ
    rows = tn * N_HEADS

    @pl.when(p == 0)
    def _init():
        q_rep = jnp.concatenate([q_ref[0]] * N_HEADS, axis=0)
        row_h = lax.broadcasted_iota(I32, (rows, ATTN_W), 0) // tn
        col_h = lax.broadcasted_iota(I32, (rows, ATTN_W), 1) // HEAD_DIM
        qb_ref[...] = jnp.where(row_h == col_h, q_rep, 0.0).astype(BF16)
        m_ref[...] = jnp.full_like(m_ref, NEG_INF)
        l_ref[...] = jnp.zeros_like(l_ref)
        acc_ref[...] = jnp.zeros_like(acc_ref)

    is_new = p == N_PAGES
    kp = jnp.where(is_new, kn_ref[0], ck_ref[0]).astype(BF16)
    vp = jnp.where(is_new, vn_ref[0], cv_ref[0]).astype(BF16)
    logits = lax.dot_general(qb_ref[...], kp, (((1,), (1,)), ((), ())), preferred_element_type=F32)
    bias = jnp.concatenate([bias_ref[0]] * N_HEADS, axis=0)
    logits = logits * ATTN_SCALE + bias
    m_old = m_ref[...]
    m_new = jnp.maximum(m_old, jnp.max(logits, axis=-1, keepdims=True))
    m_safe = jnp.where(m_new == NEG_INF, 0.0, m_new)
    pr = jnp.exp(logits - m_safe)
    alpha = jnp.exp(m_old - m_safe)
    l_ref[...] = alpha * l_ref[...] + jnp.sum(pr, axis=-1, keepdims=True)
    acc_ref[...] = alpha * acc_ref[...] + jnp.dot(pr.astype(BF16), vp, preferred_element_type=F32)
    m_ref[...] = m_new

    @pl.when(is_new)
    def _finish():
        o_full = acc_ref[...] / l_ref[...]
        for h in range(N_HEADS):
            cols = slice(h * HEAD_DIM, (h + 1) * HEAD_DIM)
            o_ref[0, :, cols] = o_full[h * tn:(h + 1) * tn, cols]


def _dsa_sample_attend(page_table, q, bias, ck, cv, k_new_pad, v_new_pad):
    rows = DEC_SEQ * N_HEADS
    page_idx = lambda b, p, pt: (pt[b * N_PAGES + jnp.minimum(p, N_PAGES - 1)], 0, 0)
    grid_spec = pltpu.PrefetchScalarGridSpec(
        num_scalar_prefetch=1,
        grid=(DEC_BATCH, N_PAGES + 1),
        in_specs=[pl.BlockSpec((1, DEC_SEQ, ATTN_W), lambda b, p, pt: (b, 0, 0)),
                  pl.BlockSpec((1, DEC_SEQ, PAGE_SIZE), lambda b, p, pt: (b, 0, p)),
                  pl.BlockSpec((1, PAGE_SIZE, ATTN_W), page_idx),
                  pl.BlockSpec((1, PAGE_SIZE, ATTN_W), page_idx),
                  pl.BlockSpec((1, PAGE_SIZE, ATTN_W), lambda b, p, pt: (b, 0, 0)),
                  pl.BlockSpec((1, PAGE_SIZE, ATTN_W), lambda b, p, pt: (b, 0, 0))],
        out_specs=pl.BlockSpec((1, DEC_SEQ, ATTN_W), lambda b, p, pt: (b, 0, 0)),
        scratch_shapes=[pltpu.VMEM((rows, ATTN_W), BF16), pltpu.VMEM((rows, 1), F32),
                        pltpu.VMEM((rows, 1), F32), pltpu.VMEM((rows, ATTN_W), F32)],
    )
    return pl.pallas_call(
        _dsa_sample_attend_kernel,
        grid_spec=grid_spec,
        out_shape=jax.ShapeDtypeStruct((DEC_BATCH, DEC_SEQ, ATTN_W), F32),
        compiler_params=_cparams(("parallel", "arbitrary")),
    )(page_table.reshape(-1), q, bias, ck, cv, k_new_pad, v_new_pad)


def _row_out(tm, rb0, n_tok_rows, width, buf):
    spec = pl.BlockSpec((tm, width), lambda i, *_: (rb0 + i, 0))
    shape = jax.ShapeDtypeStruct((n_tok_rows, width), F32)
    alias_specs = [pl.BlockSpec(memory_space=pl.ANY)] if buf is not None else []
    alias_args = [buf] if buf is not None else []
    return spec, shape, alias_specs, alias_args


def _outproj_kernel(*refs):
    o_ref, w_ref, x_ref, g_ref, b_ref, y_ref = refs[-6:]
    h = jnp.dot(o_ref[...].astype(BF16), w_ref[...], preferred_element_type=F32)
    _post_norm_store(x_ref[...], h, g_ref, b_ref, y_ref)


def _outproj_postnorm(o, w_out, x, row0, tm, g, b, buf=None):
    nrows, kdim = o.shape
    d = w_out.shape[1]
    rb0 = row0 // tm
    out_spec, out_shape, alias_specs, alias_args = _row_out(tm, rb0, x.shape[0], d, buf)
    return pl.pallas_call(
        _outproj_kernel,
        grid=(nrows // tm,),
        in_specs=alias_specs + [
            pl.BlockSpec((tm, kdim), lambda i: (i, 0)),
            pl.BlockSpec((kdim, d), lambda i: (0, 0)),
            pl.BlockSpec((tm, d), lambda i: (rb0 + i, 0)),
            pl.BlockSpec((1, d), lambda i: (0, 0)),
            pl.BlockSpec((1, d), lambda i: (0, 0))],
        out_specs=out_spec,
        out_shape=out_shape,
        input_output_aliases={0: 0} if buf is not None else {},
        compiler_params=_cparams(("parallel",)),
    )(*alias_args, o, w_out, x, g, b)


def _pool_kernel(*refs, tm, tiles_per_seq, start_pos, zero_first):
    x_ref, halo_ref, w_ref, sc_ref, g_ref, b_ref, y_ref, xe_ref, h_ref = refs[-9:]
    i = pl.program_id(0) % tiles_per_seq
    halo = halo_ref[...].reshape(HALO, D_MODEL)
    if zero_first:
        halo = jnp.where(i == 0, 0.0, halo)
    x = x_ref[...]
    xe_ref[0:HALO, :] = halo
    xe_ref[HALO:HALO + tm, :] = x
    pos = start_pos + i * tm + lax.broadcasted_iota(I32, (tm, 1), 0)
    for gi, w in enumerate(POOL_WINDOWS):
        cols = slice(gi * POOL_GROUP, (gi + 1) * POOL_GROUP)
        win = x[:, cols]
        for j in range(1, w):
            win = win + xe_ref[HALO - j:HALO - j + tm, cols]
        count = jnp.minimum(w, pos + 1).astype(F32)
        mixed = win / count - x[:, cols]
        y = jnp.dot(mixed.astype(BF16), w_ref[gi], preferred_element_type=F32)
        h_ref[:, cols] = y * sc_ref[:, cols]
    _post_norm_store(x, h_ref[...], g_ref, b_ref, y_ref)


def _pool_postnorm(x, row0, nrows, tm, tiles_per_seq, halo, halo_map, start_pos, zero_first,
                   w_pool, scale, g, b, buf=None):
    rb0 = row0 // tm
    out_spec, out_shape, alias_specs, alias_args = _row_out(tm, rb0, x.shape[0], D_MODEL, buf)
    kern = functools.partial(_pool_kernel, tm=tm, tiles_per_seq=tiles_per_seq,
                             start_pos=start_pos, zero_first=zero_first)
    halo_block = (HALO, D_MODEL) if halo.ndim == 2 else (1, HALO, D_MODEL)
    return pl.pallas_call(
        kern,
        grid=(nrows // tm,),
        in_specs=alias_specs + [
            pl.BlockSpec((tm, D_MODEL), lambda i: (rb0 + i, 0)),
            pl.BlockSpec(halo_block, halo_map),
            pl.BlockSpec((N_POOL_GROUPS, POOL_GROUP, POOL_GROUP), lambda i: (0, 0, 0)),
            pl.BlockSpec((1, D_MODEL), lambda i: (0, 0)),
            pl.BlockSpec((1, D_MODEL), lambda i: (0, 0)),
            pl.BlockSpec((1, D_MODEL), lambda i: (0, 0))],
        out_specs=out_spec,
        out_shape=out_shape,
        input_output_aliases={0: 0} if buf is not None else {},
        scratch_shapes=[pltpu.VMEM((HALO + tm, D_MODEL), F32), pltpu.VMEM((tm, D_MODEL), F32)],
        compiler_params=_cparams(("parallel",)),
    )(*alias_args, x, halo, w_pool, scale, g, b)


def _split_bf16(a):
    hi = a.astype(BF16)
    lo = (a - hi.astype(F32)).astype(BF16)
    return hi, lo


def _router_kernel(*refs):
    x_ref, w_ref, b_ref, ti_ref, gk_ref = refs[-5:]
    xh, xl = _split_bf16(x_ref[...])
    wh, wl = _split_bf16(w_ref[...])
    logits = (jnp.dot(xh, wh, preferred_element_type=F32) + jnp.dot(xh, wl, preferred_element_type=F32)
              + jnp.dot(xl, wh, preferred_element_type=F32)) + b_ref[...]
    tm = logits.shape[0]
    lane = lax.broadcasted_iota(I32, (tm, N_EXPERTS), 1).astype(F32)
    slot = lax.broadcasted_iota(I32, (tm, TOP_K), 1)
    top_v = jnp.zeros((tm, TOP_K), F32)
    top_i = jnp.zeros((tm, TOP_K), F32)
    for k in range(TOP_K):
        m = jnp.max(logits, axis=-1, keepdims=True)
        idx = jnp.min(jnp.where(logits == m, lane, float(N_EXPERTS)), axis=-1, keepdims=True)
        top_v = jnp.where(slot == k, m, top_v)
        top_i = jnp.where(slot == k, idx, top_i)
        logits = jnp.where(lane == idx, NEG_INF, logits)
    e = jnp.exp(top_v - top_v[:, 0:1])
    gk_ref[...] = e / jnp.sum(e, axis=-1, keepdims=True)
    ti_ref[...] = top_i.astype(I32)


def _router(x, row0, nrows, tm, w_r, b_r, bufs=None):
    rb0 = row0 // tm
    n = x.shape[0]
    aliased = bufs is not None
    return pl.pallas_call(
        _router_kernel,
        grid=(nrows // tm,),
        in_specs=([pl.BlockSpec(memory_space=pl.ANY)] * 2 if aliased else []) + [
            pl.BlockSpec((tm, D_MODEL), lambda i: (rb0 + i, 0)),
            pl.BlockSpec((D_MODEL, N_EXPERTS), lambda i: (0, 0)),
            pl.BlockSpec((1, N_EXPERTS), lambda i: (0, 0))],
        out_specs=[pl.BlockSpec((tm, TOP_K), lambda i: (rb0 + i, 0))] * 2,
        out_shape=[jax.ShapeDtypeStruct((n, TOP_K), I32), jax.ShapeDtypeStruct((n, TOP_K), F32)],
        input_output_aliases={0: 0, 1: 1} if aliased else {},
        compiler_params=_cparams(("parallel",)),
    )(*(list(bufs) if aliased else []), x, w_r, b_r)


def _moe_plan(top_i):
    flat = top_i.reshape(N_ASSIGN)
    onehot = (flat[:, None] == jnp.arange(N_EXPERTS, dtype=I32)[None, :]).astype(I32)
    csum = jnp.cumsum(onehot, axis=0)
    rank = jnp.sum(csum * onehot, axis=1) - 1
    counts = csum[-1]
    tiles = (counts + MOE_TM - 1) // MOE_TM
    tile_end = jnp.cumsum(tiles)
    tile_start = tile_end - tiles
    dest = tile_start[flat] * MOE_TM + rank
    row_token = jnp.zeros((MOE_P,), I32).at[dest].set(jnp.arange(N_ASSIGN, dtype=I32) // TOP_K)
    groups = (tiles + MOE_RT - 1) // MOE_RT
    group_end = jnp.cumsum(groups)
    n_groups = group_end[-1]
    gid = jnp.arange(MOE_G, dtype=I32)
    g_eff = jnp.minimum(gid, n_groups - 1)
    g_exp = jnp.minimum(jnp.searchsorted(group_end, g_eff, side="right").astype(I32), N_EXPERTS - 1)
    local = g_eff - (group_end - groups)[g_exp]
    g_tile0 = tile_start[g_exp] + local * MOE_RT
    g_nt = jnp.where(gid < n_groups, jnp.clip(tiles[g_exp] - local * MOE_RT, 0, MOE_RT), 0)
    return (dest.astype(I32), row_token, g_exp, g_tile0.astype(I32), g_nt.astype(I32),
            n_groups.reshape(1).astype(I32))


def _moe_expert_kernel(gexp_ref, gtile0_ref, gnt_ref, ng_ref, tok_ref,
                       x_hbm, wg_ref, wu_ref, bg_ref, bu_ref, wd_ref, bd_ref, y_hbm,
                       xg_ref, acc_ref, gsem, ssem):
    g = pl.program_id(0)
    c = pl.program_id(1)
    n_chunks = pl.num_programs(1)
    nt = gnt_ref[g]
    row0 = gtile0_ref[g] * MOE_TM
    n_rows = nt * MOE_TM

    def row_copy(r):
        return pltpu.make_async_copy(x_hbm.at[pl.ds(tok_ref[row0 + r], 1)], xg_ref.at[pl.ds(r, 1)], gsem)

    @pl.when((nt > 0) & (c == 0))
    def _gather():
        lax.fori_loop(0, n_rows, lambda r, _: (row_copy(r).start(), 0)[1], 0)
        lax.fori_loop(0, n_rows, lambda r, _: (row_copy(r).wait(), 0)[1], 0)

    @pl.when(nt > 0)
    def _compute():
        wg = wg_ref[0].astype(BF16)
        wu = wu_ref[0].astype(BF16)
        wd = wd_ref[0].astype(BF16)

        def tile(t, _):
            r0 = pl.multiple_of(t * MOE_TM, MOE_TM)
            xt = xg_ref[pl.ds(r0, MOE_TM), :].astype(BF16)
            hg = jnp.dot(xt, wg, preferred_element_type=F32) + bg_ref[0]
            hu = jnp.dot(xt, wu, preferred_element_type=F32) + bu_ref[0]
            gate = jnp.minimum(hg, SWIGLU_LIMIT)
            lin = jnp.clip(hu, -SWIGLU_LIMIT, SWIGLU_LIMIT)
            act = gate * jax.nn.sigmoid(SWIGLU_ALPHA * gate) * (lin + 1.0)
            contrib = jnp.dot(act.astype(BF16), wd, preferred_element_type=F32)

            @pl.when(c == 0)
            def _first():
                acc_ref[pl.ds(r0, MOE_TM), :] = contrib + bd_ref[0]

            @pl.when(c > 0)
            def _rest():
                acc_ref[pl.ds(r0, MOE_TM), :] += contrib

            return 0

        lax.fori_loop(0, nt, tile, 0)

    def tile_store(t):
        r0 = pl.multiple_of(t * MOE_TM, MOE_TM)
        return pltpu.make_async_copy(acc_ref.at[pl.ds(r0, MOE_TM)],
                                     y_hbm.at[pl.ds(pl.multiple_of(row0 + r0, MOE_TM), MOE_TM)], ssem)

    @pl.when((nt > 0) & (c == n_chunks - 1))
    def _store():
        lax.fori_loop(0, nt, lambda t, _: (tile_store(t).start(), 0)[1], 0)
        lax.fori_loop(0, nt, lambda t, _: (tile_store(t).wait(), 0)[1], 0)


def _moe_experts(x, plan, w_gu, b_gu, w_d, b_d):
    _, row_token, g_exp, g_tile0, g_nt, n_groups = plan
    n_chunks = D_FF // MOE_F

    def chunk(g, c, ng):
        return jnp.where(g < ng[0], c, n_chunks - 1)

    grid_spec = pltpu.PrefetchScalarGridSpec(
        num_scalar_prefetch=5,
        grid=(MOE_G, n_chunks),
        in_specs=[
            pl.BlockSpec(memory_space=pl.ANY),
            pl.BlockSpec((1, D_MODEL, MOE_F), lambda g, c, ge, gt, gn, ng, tok: (ge[g], 0, chunk(g, c, ng))),
            pl.BlockSpec((1, D_MODEL, MOE_F),
                         lambda g, c, ge, gt, gn, ng, tok: (ge[g], 0, n_chunks + chunk(g, c, ng))),
            pl.BlockSpec((1, 1, MOE_F), lambda g, c, ge, gt, gn, ng, tok: (ge[g], 0, chunk(g, c, ng))),
            pl.BlockSpec((1, 1, MOE_F), lambda g, c, ge, gt, gn, ng, tok: (ge[g], 0, n_chunks + chunk(g, c, ng))),
            pl.BlockSpec((1, MOE_F, D_MODEL), lambda g, c, ge, gt, gn, ng, tok: (ge[g], chunk(g, c, ng), 0)),
            pl.BlockSpec((1, 1, D_MODEL), lambda g, c, ge, gt, gn, ng, tok: (ge[g], 0, 0)),
        ],
        out_specs=pl.BlockSpec(memory_space=pl.ANY),
        scratch_shapes=[pltpu.VMEM((MOE_RT * MOE_TM, D_MODEL), F32),
                        pltpu.VMEM((MOE_RT * MOE_TM, D_MODEL), F32),
                        pltpu.SemaphoreType.DMA(()), pltpu.SemaphoreType.DMA(())],
    )
    return pl.pallas_call(
        _moe_expert_kernel,
        grid_spec=grid_spec,
        out_shape=jax.ShapeDtypeStruct((MOE_P, D_MODEL), F32),
        compiler_params=_cparams(("arbitrary", "arbitrary")),
    )(g_exp, g_tile0, g_nt, n_groups, row_token, x, w_gu, w_gu,
      b_gu.reshape(N_EXPERTS, 1, 2 * D_FF), b_gu.reshape(N_EXPERTS, 1, 2 * D_FF),
      w_d, b_d.reshape(N_EXPERTS, 1, D_MODEL))


def _moe_combine_kernel(dest_ref, *refs, tm, row0):
    y_hbm, gk_ref, x_ref, g_ref, b_ref, o_ref, ybuf_ref, sem = refs[-8:]
    base = (row0 + pl.program_id(0) * tm) * TOP_K

    def row_copy(a):
        r, k = lax.shift_right_logical(a, TOP_K_SHIFT), a & (TOP_K - 1)
        return pltpu.make_async_copy(y_hbm.at[pl.ds(dest_ref[base + a], 1)], ybuf_ref.at[k, pl.ds(r, 1)], sem)

    lax.fori_loop(0, tm * TOP_K, lambda a, _: (row_copy(a).start(), 0)[1], 0)
    lax.fori_loop(0, tm * TOP_K, lambda a, _: (row_copy(a).wait(), 0)[1], 0)
    gk = gk_ref[...]
    out = gk[:, 0:1] * ybuf_ref[0]
    for k in range(1, TOP_K):
        out = out + gk[:, k:k + 1] * ybuf_ref[k]
    _post_norm_store(x_ref[...], out, g_ref, b_ref, o_ref)


def _moe_combine_postnorm(y_sorted, dest, gk, x, row0, nrows, tm, g, b, buf=None):
    rb0 = row0 // tm
    out_spec, out_shape, alias_specs, alias_args = _row_out(tm, rb0, x.shape[0], D_MODEL, buf)
    kern = functools.partial(_moe_combine_kernel, tm=tm, row0=row0)
    grid_spec = pltpu.PrefetchScalarGridSpec(
        num_scalar_prefetch=1,
        grid=(nrows // tm,),
        in_specs=alias_specs + [
            pl.BlockSpec(memory_space=pl.ANY),
            pl.BlockSpec((tm, TOP_K), lambda i, d: (rb0 + i, 0)),
            pl.BlockSpec((tm, D_MODEL), lambda i, d: (rb0 + i, 0)),
            pl.BlockSpec((1, D_MODEL), lambda i, d: (0, 0)),
            pl.BlockSpec((1, D_MODEL), lambda i, d: (0, 0))],
        out_specs=out_spec,
        scratch_shapes=[pltpu.VMEM((TOP_K, tm, D_MODEL), F32), pltpu.SemaphoreType.DMA(())],
    )
    return pl.pallas_call(
        kern,
        grid_spec=grid_spec,
        out_shape=out_shape,
        input_output_aliases={1: 0} if buf is not None else {},
        compiler_params=_cparams(("arbitrary",)),
    )(dest, *alias_args, y_sorted, gk, x, g, b)


def _moe_postnorm(x, w_r, b_r, w_gu, b_gu, w_d, b_d, g, b):
    routed = _router(x, 0, N_PROMPT, TM_PROMPT, w_r, b_r)
    top_i, gk = _router(x, N_PROMPT, N_SAMPLE, TM_SAMPLE, w_r, b_r, bufs=routed)
    plan = _moe_plan(top_i)
    y_sorted = _moe_experts(x, plan, w_gu, b_gu, w_d, b_d)
    buf = _moe_combine_postnorm(y_sorted, plan[0], gk, x, 0, N_PROMPT, TM_PROMPT, g, b)
    return _moe_combine_postnorm(y_sorted, plan[0], gk, x, N_PROMPT, N_SAMPLE, TM_SAMPLE, g, b, buf=buf)


def _attn_layer(x, cache_k, cache_v, cache_kidx, page_table, w_in, kn_g, kn_b, w_out, g, b):
    w_in_bf = w_in.astype(BF16)
    w_out_bf = w_out.astype(BF16)
    w_kiwi = jnp.pad(w_in_bf[:, OFF_KI:], ((0, 0), (0, 2 * LANES - (IN_W - OFF_KI))))
    kn_g = kn_g.reshape(1, IDX_DIM)
    kn_b = kn_b.reshape(1, IDX_DIM)

    def project(row0, nrows, tm, q_dtype):
        tm_mm = min(TM_MM, nrows)
        (q,) = _mm(x, row0, nrows, tm_mm, w_in_bf, 0, ATTN_W, (q_dtype,))
        k, kbf = _mm(x, row0, nrows, tm_mm, w_in_bf, OFF_K, ATTN_W, (F32, BF16))
        v, vbf = _mm(x, row0, nrows, tm_mm, w_in_bf, OFF_V, ATTN_W, (F32, BF16))
        (qi,) = _mm(x, row0, nrows, tm_mm, w_in_bf, OFF_QI, IDX_W, (BF16,))
        ki, wi = _kiwi(x, row0, nrows, tm, w_kiwi, kn_g, kn_b)
        return q, k, kbf, v, vbf, qi, ki, wi

    q, k_p, kbf, v_p, vbf, qi, ki_p, wi = project(0, N_PROMPT, TM_PROMPT, BF16)
    s3 = lambda a: a.reshape(BATCH, SEQ, a.shape[-1])
    o_p = _dsa_prompt(s3(q), s3(qi), s3(wi), s3(ki_p), s3(kbf), s3(vbf)).reshape(N_PROMPT, ATTN_W)

    q, k_s, _, v_s, _, qi, ki_s, wi = project(N_PROMPT, N_SAMPLE, TM_SAMPLE, F32)
    pad_rows = lambda a: jnp.pad(a.reshape(DEC_BATCH, DEC_SEQ, a.shape[-1]),
                                 ((0, 0), (0, PAGE_SIZE - DEC_SEQ), (0, 0)))
    bias = _dsa_sample_select(page_table,
                              qi.reshape(DEC_BATCH, DEC_SEQ * N_IDX_HEADS, IDX_DIM),
                              wi.reshape(DEC_BATCH, DEC_SEQ * N_IDX_HEADS, 1),
                              cache_kidx, pad_rows(ki_s))
    pages = cache_k.shape[0]
    o_s = _dsa_sample_attend(page_table, q.reshape(DEC_BATCH, DEC_SEQ, ATTN_W), bias,
                             cache_k.reshape(pages, PAGE_SIZE, ATTN_W), cache_v.reshape(pages, PAGE_SIZE, ATTN_W),
                             pad_rows(k_s), pad_rows(v_s)).reshape(N_SAMPLE, ATTN_W)

    buf = _outproj_postnorm(o_p, w_out_bf, x, 0, TM_PROMPT, g, b)
    x1 = _outproj_postnorm(o_s, w_out_bf, x, N_PROMPT, TM_SAMPLE, g, b, buf=buf)
    return x1, (k_p, v_p, ki_p, k_s, v_s, ki_s)


def _pool_layer(x, state, w_pool, scale, g, b):
    w_bf = w_pool.astype(BF16)
    scale = scale.reshape(1, D_MODEL)
    tiles = SEQ // TM_PROMPT
    hb = TM_PROMPT // HALO
    buf = _pool_postnorm(x, 0, N_PROMPT, TM_PROMPT, tiles, x,
                         lambda i: (jnp.maximum(i * hb - 1, 0), 0), 0, True, w_bf, scale, g, b)
    hist = jnp.pad(state, ((0, 0), (HALO - POOL_HIST, 0), (0, 0)))
    x1 = _pool_postnorm(x, N_PROMPT, N_SAMPLE, DEC_SEQ, 1, hist,
                        lambda i: (i, 0, 0), PAST_LEN, False, w_bf, scale, g, b, buf=buf)
    xp = x[:N_PROMPT].reshape(BATCH, SEQ, D_MODEL)
    xs = x[N_PROMPT:].reshape(DEC_BATCH, DEC_SEQ, D_MODEL)
    new_state_p = xp[:, SEQ - POOL_HIST:]
    new_state_s = jnp.concatenate([state, xs], axis=1)[:, -POOL_HIST:]
    return x1, (new_state_p, new_state_s)


def kernel(x_prompt, x_sample, cache_k, cache_v, cache_kidx, state_pool, page_table, w_attn_in, idx_knorm_g, idx_knorm_b, w_attn_out, w_pool, pool_scale, ln_g, ln_b, w_router, b_router, w_gate_up, b_gate_up, w_down, b_down):
    x = jnp.concatenate([x_prompt.reshape(N_PROMPT, D_MODEL), x_sample.reshape(N_SAMPLE, D_MODEL)], axis=0)
    attn_rows, pool_rows = [], []
    for i in range(DEPTH):
        j = i // N_MIXERS
        g0, b0 = ln_g[i, 0].reshape(1, D_MODEL), ln_b[i, 0].reshape(1, D_MODEL)
        g1, b1 = ln_g[i, 1].reshape(1, D_MODEL), ln_b[i, 1].reshape(1, D_MODEL)
        if i % N_MIXERS == 0:
            x, rows = _attn_layer(x, cache_k[j], cache_v[j], cache_kidx[j], page_table,
                                  w_attn_in[j], idx_knorm_g[j], idx_knorm_b[j], w_attn_out[j], g0, b0)
            attn_rows.append(rows)
        else:
            x, rows = _pool_layer(x, state_pool[j], w_pool[j], pool_scale[j], g0, b0)
            pool_rows.append(rows)
        x = _moe_postnorm(x, w_router[i], b_router[i].reshape(1, N_EXPERTS), w_gate_up[i], b_gate_up[i],
                          w_down[i], b_down[i], g1, b1)

    def stack(rows, idx, lead, tail):
        return jnp.stack([r[idx].reshape(lead + tail) for r in rows])

    hd = (N_HEADS, HEAD_DIM)
    return (x[:N_PROMPT].reshape(BATCH, SEQ, D_MODEL),
            x[N_PROMPT:].reshape(DEC_BATCH, DEC_SEQ, D_MODEL),
            stack(attn_rows, 0, (BATCH, SEQ), hd), stack(attn_rows, 1, (BATCH, SEQ), hd),
            stack(attn_rows, 2, (BATCH, SEQ), (IDX_DIM,)),
            jnp.stack([r[0] for r in pool_rows]),
            stack(attn_rows, 3, (DEC_BATCH, DEC_SEQ), hd), stack(attn_rows, 4, (DEC_BATCH, DEC_SEQ), hd),
            stack(attn_rows, 5, (DEC_BATCH, DEC_SEQ), (IDX_DIM,)),
            jnp.stack([r[1] for r in pool_rows]))
```

```python
import functools

import jax
import jax.numpy as jnp
from jax import lax
from jax.experimental import pallas as pl
from jax.experimental.pallas import tpu as pltpu

D_MODEL = 2048
BATCH = 4
SEQ = 2048
DEPTH = 4
DEC_BATCH = 8
DEC_SEQ = 8
PAST_LEN = 16384
PAGE_SIZE = 128

N_MIXERS = 2
N_HEADS = 16
HEAD_DIM = 128
ATTN_W = N_HEADS * HEAD_DIM
N_IDX_HEADS = 16
IDX_DIM = 128
IDX_W = N_IDX_HEADS * IDX_DIM
TOPK_MAX = 256
ATTN_SCALE = HEAD_DIM ** -0.5
IDX_SCALE = IDX_DIM ** -0.5
OFF_K = ATTN_W
OFF_V = 2 * ATTN_W
OFF_QI = 3 * ATTN_W
OFF_KI = OFF_QI + IDX_W
OFF_WI = OFF_KI + IDX_DIM
IN_W = OFF_WI + N_IDX_HEADS
POOL_WINDOWS = (2, 4, 8, 16)
N_POOL_GROUPS = len(POOL_WINDOWS)
POOL_GROUP = D_MODEL // N_POOL_GROUPS
POOL_HIST = max(POOL_WINDOWS) - 1
N_EXPERTS = 32
TOP_K = 4
D_FF = D_MODEL
SWIGLU_LIMIT = 7.0
SWIGLU_ALPHA = 1.702
LN_EPS = 1e-5
DEEPNORM_ALPHA = (2 * DEPTH) ** 0.25

N_PROMPT = BATCH * SEQ
N_SAMPLE = DEC_BATCH * DEC_SEQ
N_TOK = N_PROMPT + N_SAMPLE
N_PAGES = PAST_LEN // PAGE_SIZE

LANES = 128
HALO = 16
VMEM_LIMIT = 56 * 1024 * 1024

TM_PROMPT = min(256, N_PROMPT)
TM_SAMPLE = N_SAMPLE
TM_MM = min(1024, N_PROMPT)
TN_MM = min(512, ATTN_W)
TQ = min(256, SEQ)
SEL_G = min(8, N_PAGES)
MOE_TM = min(256, N_PROMPT)
MOE_RT = 5
MOE_F = min(256, D_FF)
TOP_K_SHIFT = TOP_K.bit_length() - 1
PAGE_SHIFT = PAGE_SIZE.bit_length() - 1
N_ASSIGN = N_TOK * TOP_K
MOE_P = (N_ASSIGN // MOE_TM + N_EXPERTS) * MOE_TM
MOE_G = (N_ASSIGN // MOE_TM + N_EXPERTS) // MOE_RT + N_EXPERTS + 1

F32 = jnp.float32
BF16 = jnp.bfloat16
I32 = jnp.int32
NEG_INF = float("-inf")
INT_MIN = -2 ** 31

assert 1 << TOP_K_SHIFT == TOP_K and 1 << PAGE_SHIFT == PAGE_SIZE and N_PAGES % SEL_G == 0


def _cparams(sem):
    return pltpu.CompilerParams(dimension_semantics=sem, vmem_limit_bytes=VMEM_LIMIT)


def _layer_norm(x, g, b):
    mu = jnp.mean(x, axis=-1, keepdims=True)
    xc = x - mu
    var = jnp.mean(xc * xc, axis=-1, keepdims=True)
    return xc * lax.rsqrt(var + LN_EPS) * g + b


def _post_norm_store(x, h, g_ref, b_ref, o_ref):
    o_ref[...] = _layer_norm(DEEPNORM_ALPHA * x + h, g_ref[...], b_ref[...])


def _mm_kernel(x_ref, w_ref, *o_refs):
    acc = jnp.dot(x_ref[...].astype(BF16), w_ref[...], preferred_element_type=F32)
    for o in o_refs:
        o[...] = acc.astype(o.dtype)


def _mm(x, row0, nrows, tm, w, col0, ncols, out_dtypes):
    k = x.shape[1]
    tn = min(TN_MM, ncols)
    rb0, cb0 = row0 // tm, col0 // tn
    return pl.pallas_call(
        _mm_kernel,
        grid=(nrows // tm, ncols // tn),
        in_specs=[pl.BlockSpec((tm, k), lambda i, j: (rb0 + i, 0)),
                  pl.BlockSpec((k, tn), lambda i, j: (0, cb0 + j))],
        out_specs=[pl.BlockSpec((tm, tn), lambda i, j: (i, j)) for _ in out_dtypes],
        out_shape=[jax.ShapeDtypeStruct((nrows, ncols), dt) for dt in out_dtypes],
        compiler_params=_cparams(("parallel", "arbitrary")),
    )(x, w)


def _kiwi_kernel(x_ref, w_ref, g_ref, b_ref, ki_ref, wi_ref):
    p = jnp.dot(x_ref[...].astype(BF16), w_ref[...], preferred_element_type=F32)
    ki_ref[...] = _layer_norm(p[:, :IDX_DIM], g_ref[...], b_ref[...])
    wi_ref[...] = p[:, IDX_DIM:IDX_DIM + N_IDX_HEADS] * (N_IDX_HEADS ** -0.5)


def _kiwi(x, row0, nrows, tm, w_kiwi, g, b):
    k = x.shape[1]
    rb0 = row0 // tm
    wcols = w_kiwi.shape[1]
    return pl.pallas_call(
        _kiwi_kernel,
        grid=(nrows // tm,),
        in_specs=[pl.BlockSpec((tm, k), lambda i: (rb0 + i, 0)),
                  pl.BlockSpec((k, wcols), lambda i: (0, 0)),
                  pl.BlockSpec((1, IDX_DIM), lambda i: (0, 0)),
                  pl.BlockSpec((1, IDX_DIM), lambda i: (0, 0))],
        out_specs=[pl.BlockSpec((tm, IDX_DIM), lambda i: (i, 0)),
                   pl.BlockSpec((tm, N_IDX_HEADS), lambda i: (i, 0))],
        out_shape=[jax.ShapeDtypeStruct((nrows, IDX_DIM), F32),
                   jax.ShapeDtypeStruct((nrows, N_IDX_HEADS), F32)],
        compiler_params=_cparams(("parallel",)),
    )(x, w_kiwi, g, b)


def _count(mask):
    return jnp.sum(jnp.where(mask, 1.0, 0.0), axis=-1, keepdims=True)


def _kth_largest_key(key, k):
    lo = jnp.where(_count(key >= 0) >= k, 0, INT_MIN).astype(I32)

    def body(i, lo):
        cand = lo + jnp.left_shift(jnp.int32(1), 30 - i)
        return jnp.where(_count(key >= cand) >= k, cand, lo)

    return lax.fori_loop(0, 31, body, lo)


def _strictly_lower():
    before = lax.broadcasted_iota(I32, (LANES, LANES), 0) < lax.broadcasted_iota(I32, (LANES, LANES), 1)
    return jnp.where(before, 1.0, 0.0).astype(BF16)


def _select_bias(score, adm, k, key_ref, bias_ref):
    score = jnp.where(score == 0.0, 0.0, score)
    bits = pltpu.bitcast(score, I32)
    key = jnp.where(bits < 0, bits ^ jnp.int32(0x7FFFFFFF), bits)
    key = jnp.where(adm, key, INT_MIN)
    thr = jnp.maximum(_kth_largest_key(key, k), INT_MIN + 1)
    ge = key >= thr
    bias_ref[...] = jnp.where(ge, 0.0, NEG_INF)
    n_ge = _count(ge)

    @pl.when(jnp.max(n_ge) > k)
    def _ties():
        key_ref[...] = key
        need = k - _count(key > thr)
        tri = _strictly_lower()

        def tile(j, carry):
            off = pl.multiple_of(j * LANES, LANES)
            kj = key_ref[:, pl.ds(off, LANES)]
            eq = kj == thr
            e = jnp.where(eq, 1.0, 0.0)
            rank = jnp.dot(e.astype(BF16), tri, preferred_element_type=F32) + carry
            sel = (kj > thr) | (eq & (rank < need))
            bias_ref[:, pl.ds(off, LANES)] = jnp.where(sel, 0.0, NEG_INF)
            return carry + jnp.sum(e, axis=-1, keepdims=True)

        lax.fori_loop(0, key_ref.shape[1] // LANES, tile, jnp.zeros_like(need))


def _dsa_prompt_kernel(q_ref, qi_ref, wi_ref, ki_ref, k_ref, v_ref, o_ref, key_ref, bias_ref, *, n_sel):
    i = pl.program_id(1)
    s_len = ki_ref.shape[1]
    ki = ki_ref[0].astype(BF16)
    wi = wi_ref[0]
    score = jnp.zeros((TQ, s_len), F32)
    for h in range(N_IDX_HEADS):
        d = lax.dot_general(qi_ref[0, :, h * IDX_DIM:(h + 1) * IDX_DIM], ki,
                            (((1,), (1,)), ((), ())), preferred_element_type=F32)
        score = score + jnp.maximum(d, 0.0) * wi[:, h:h + 1]
    score = score * IDX_SCALE
    t_pos = i * TQ + lax.broadcasted_iota(I32, (TQ, s_len), 0)
    adm = lax.broadcasted_iota(I32, (TQ, s_len), 1) <= t_pos
    _select_bias(jnp.where(adm, score, NEG_INF), adm, n_sel, key_ref, bias_ref)

    for h in range(N_HEADS):
        cols = slice(h * HEAD_DIM, (h + 1) * HEAD_DIM)
        logits = lax.dot_general(q_ref[0, :, cols], k_ref[0, :, cols],
                                 (((1,), (1,)), ((), ())), preferred_element_type=F32)
        logits = logits * ATTN_SCALE + bias_ref[...]
        m = jnp.max(logits, axis=-1, keepdims=True)
        p = jnp.exp(logits - m)
        denom = jnp.sum(p, axis=-1, keepdims=True)
        o = jnp.dot(p.astype(BF16), v_ref[0, :, cols], preferred_element_type=F32)
        o_ref[0, :, cols] = (o / denom).astype(o_ref.dtype)


def _dsa_prompt(q, qi, wi, ki, k, v):
    b, t = q.shape[:2]
    n_sel = min(TOPK_MAX, t // 4)
    kern = functools.partial(_dsa_prompt_kernel, n_sel=n_sel)
    once = pl.Buffered(1)
    return pl.pallas_call(
        kern,
        grid=(b, t // TQ),
        in_specs=[pl.BlockSpec((1, TQ, ATTN_W), lambda bb, i: (bb, i, 0)),
                  pl.BlockSpec((1, TQ, IDX_W), lambda bb, i: (bb, i, 0)),
                  pl.BlockSpec((1, TQ, N_IDX_HEADS), lambda bb, i: (bb, i, 0)),
                  pl.BlockSpec((1, t, IDX_DIM), lambda bb, i: (bb, 0, 0), pipeline_mode=once),
                  pl.BlockSpec((1, t, ATTN_W), lambda bb, i: (bb, 0, 0), pipeline_mode=once),
                  pl.BlockSpec((1, t, ATTN_W), lambda bb, i: (bb, 0, 0), pipeline_mode=once)],
        out_specs=pl.BlockSpec((1, TQ, ATTN_W), lambda bb, i: (bb, i, 0)),
        out_shape=jax.ShapeDtypeStruct((b, t, ATTN_W), BF16),
        scratch_shapes=[pltpu.VMEM((TQ, t), I32), pltpu.VMEM((TQ, t), F32)],
        compiler_params=_cparams(("parallel", "arbitrary")),
    )(q, qi, wi, ki, k, v)


def _dsa_sample_select_kernel(pt_ref, qi_ref, wi_ref, *refs, n_sel):
    cki_refs = refs[:SEL_G]
    kin_ref, idx_ref, cnt_ref, score_ref, key_ref, bias_ref, hit_ref = refs[SEL_G:]
    p = pl.program_id(1)
    tn = DEC_SEQ
    n_steps = N_PAGES // SEL_G
    s_len = score_ref.shape[1]

    def page_score(kpage, off):
        d = lax.dot_general(qi_ref[0], kpage.astype(BF16), (((1,), (1,)), ((), ())), preferred_element_type=F32)
        r = jnp.maximum(d, 0.0) * wi_ref[0]
        score_ref[:, pl.ds(off, PAGE_SIZE)] = jnp.sum(r.reshape(tn, N_IDX_HEADS, PAGE_SIZE), axis=1) * IDX_SCALE

    @pl.when(p < n_steps)
    def _cached_pages():
        for gi in range(SEL_G):
            page_score(cki_refs[gi][0, 0], pl.multiple_of((p * SEL_G + gi) * PAGE_SIZE, PAGE_SIZE))

    @pl.when(p == n_steps)
    def _new_keys_then_select():
        page_score(kin_ref[0], PAST_LEN)
        q_pos = PAST_LEN + lax.broadcasted_iota(I32, (tn, s_len), 0)
        adm = lax.broadcasted_iota(I32, (tn, s_len), 1) <= q_pos
        _select_bias(jnp.where(adm, score_ref[...], NEG_INF), adm, n_sel, key_ref, bias_ref)

        hit_ref[...] = jnp.zeros_like(hit_ref)
        tri = _strictly_lower()
        slot = lax.broadcasted_iota(I32, (n_sel, LANES), 0).astype(F32)
        lane = lax.broadcasted_iota(I32, (1, LANES), 1)

        def tile(t, carry):
            off = pl.multiple_of(t * LANES, LANES)
            e = jnp.where(bias_ref[:, pl.ds(off, LANES)] == 0.0, 1.0, 0.0)
            rank = jnp.dot(e.astype(BF16), tri, preferred_element_type=F32) + carry
            pos = (lane + off).astype(F32)
            for q in range(tn):
                hit = (rank[q:q + 1, :] == slot) & (e[q:q + 1, :] > 0.0)
                hit_ref[q] += jnp.where(hit, pos, 0.0)
            return carry + jnp.sum(e, axis=-1, keepdims=True)

        cnt = lax.fori_loop(0, s_len // LANES, tile, jnp.zeros((tn, 1), F32))
        for q in range(tn):
            idx_ref[0, q * n_sel:(q + 1) * n_sel, :] = jnp.sum(hit_ref[q], axis=-1, keepdims=True).astype(I32)
        cnt_ref[0] = cnt.astype(I32)


def _dsa_sample_select(page_table, layer, qi_flat, wi_flat, cki, ki_new_pad):
    s_len = PAST_LEN + PAGE_SIZE
    n_sel = min(TOPK_MAX, (PAST_LEN + DEC_SEQ) // 4)
    rows = DEC_SEQ * N_IDX_HEADS
    n_steps = N_PAGES // SEL_G
    kern = functools.partial(_dsa_sample_select_kernel, n_sel=n_sel)

    def page_spec(gi):
        def index(b, p, pt):
            return (layer, pt[b * N_PAGES + jnp.minimum(p, n_steps - 1) * SEL_G + gi], 0, 0)
        return pl.BlockSpec((1, 1, PAGE_SIZE, IDX_DIM), index)

    grid_spec = pltpu.PrefetchScalarGridSpec(
        num_scalar_prefetch=1,
        grid=(DEC_BATCH, n_steps + 1),
        in_specs=[pl.BlockSpec((1, rows, IDX_DIM), lambda b, p, pt: (b, 0, 0)),
                  pl.BlockSpec((1, rows, 1), lambda b, p, pt: (b, 0, 0))]
                 + [page_spec(gi) for gi in range(SEL_G)]
                 + [pl.BlockSpec((1, PAGE_SIZE, IDX_DIM), lambda b, p, pt: (b, 0, 0))],
        out_specs=[pl.BlockSpec((1, DEC_SEQ * n_sel, 1), lambda b, p, pt: (b, 0, 0)),
                   pl.BlockSpec((1, DEC_SEQ, 1), lambda b, p, pt: (b, 0, 0))],
        scratch_shapes=[pltpu.VMEM((DEC_SEQ, s_len), F32), pltpu.VMEM((DEC_SEQ, s_len), I32),
                        pltpu.VMEM((DEC_SEQ, s_len), F32), pltpu.VMEM((DEC_SEQ, n_sel, LANES), F32)],
    )
    idx, cnt = pl.pallas_call(
        kern,
        grid_spec=grid_spec,
        out_shape=[jax.ShapeDtypeStruct((DEC_BATCH, DEC_SEQ * n_sel, 1), I32),
                   jax.ShapeDtypeStruct((DEC_BATCH, DEC_SEQ, 1), I32)],
        compiler_params=_cparams(("parallel", "arbitrary")),
    )(page_table.reshape(-1), qi_flat, wi_flat, *([cki] * SEL_G), ki_new_pad)
    return idx.reshape(-1), cnt.reshape(-1), n_sel


def _dsa_sample_attend_kernel(idx_ref, cnt_ref, pt_ref, q_ref, ck_hbm, cv_hbm, kn_hbm, vn_hbm, o_ref,
                              kbuf, vbuf, ksem, vsem, *, layer, n_sel):
    i = pl.program_id(0)
    slot = i & 1

    def fetch(step, slot):
        b = lax.div(step, DEC_SEQ)

        def row(j, _):
            s = idx_ref[step * n_sel + j]

            @pl.when(s < PAST_LEN)
            def _cached():
                page = pt_ref[b * N_PAGES + lax.shift_right_logical(s, PAGE_SHIFT)]
                r = s & (PAGE_SIZE - 1)
                pltpu.make_async_copy(ck_hbm.at[layer, page, r], kbuf.at[slot, j], ksem.at[slot]).start()
                pltpu.make_async_copy(cv_hbm.at[layer, page, r], vbuf.at[slot, j], vsem.at[slot]).start()

            @pl.when(s >= PAST_LEN)
            def _new():
                n = b * DEC_SEQ + (s - PAST_LEN)
                pltpu.make_async_copy(kn_hbm.at[n], kbuf.at[slot, j], ksem.at[slot]).start()
                pltpu.make_async_copy(vn_hbm.at[n], vbuf.at[slot, j], vsem.at[slot]).start()

            return 0

        lax.fori_loop(0, n_sel, row, 0)

    def wait(slot):
        def row(j, _):
            pltpu.make_async_copy(kn_hbm.at[0], kbuf.at[slot, j], ksem.at[slot]).wait()
            pltpu.make_async_copy(vn_hbm.at[0], vbuf.at[slot, j], vsem.at[slot]).wait()
            return 0

        lax.fori_loop(0, n_sel, row, 0)

    @pl.when(i == 0)
    def _first():
        fetch(i, slot)

    wait(slot)

    @pl.when(i + 1 < pl.num_programs(0))
    def _prefetch():
        fetch(i + 1, 1 - slot)

    k = kbuf[slot]
    logits = jnp.sum(k * q_ref[...], axis=-1, keepdims=True) * ATTN_SCALE
    valid = lax.broadcasted_iota(I32, logits.shape, 0) < cnt_ref[i]
    logits = jnp.where(valid, logits, NEG_INF)
    m = jnp.max(logits, axis=0, keepdims=True)
    p = jnp.exp(logits - m)
    denom = jnp.sum(p, axis=0, keepdims=True)
    o_ref[...] = jnp.sum(p * vbuf[slot], axis=0, keepdims=True) / denom


def _dsa_sample_attend(idx, cnt, n_sel, page_table, layer, q, ck, cv, k_new, v_new):
    n = q.shape[0]
    grid_spec = pltpu.PrefetchScalarGridSpec(
        num_scalar_prefetch=3,
        grid=(n,),
        in_specs=[pl.BlockSpec((1, N_HEADS, HEAD_DIM), lambda i, *_: (i, 0, 0))]
                 + [pl.BlockSpec(memory_space=pl.ANY)] * 4,
        out_specs=pl.BlockSpec((1, N_HEADS, HEAD_DIM), lambda i, *_: (i, 0, 0)),
        scratch_shapes=[pltpu.VMEM((2, n_sel, N_HEADS, HEAD_DIM), F32),
                        pltpu.VMEM((2, n_sel, N_HEADS, HEAD_DIM), F32),
                        pltpu.SemaphoreType.DMA((2,)), pltpu.SemaphoreType.DMA((2,))],
    )
    return pl.pallas_call(
        functools.partial(_dsa_sample_attend_kernel, layer=layer, n_sel=n_sel),
        grid_spec=grid_spec,
        out_shape=jax.ShapeDtypeStruct((n, N_HEADS, HEAD_DIM), F32),
        compiler_params=_cparams(("arbitrary",)),
    )(idx, cnt, page_table.reshape(-1), q, ck, cv, k_new, v_new)


def _row_out(tm, rb0, n_tok_rows, width, buf):
    spec = pl.BlockSpec((tm, width), lambda i, *_: (rb0 + i, 0))
    shape = jax.ShapeDtypeStruct((n_tok_rows, width), F32)
    alias_specs = [pl.BlockSpec(memory_space=pl.ANY)] if buf is not None else []
    alias_args = [buf] if buf is not None else []
    return spec, shape, alias_specs, alias_args


def _outproj_kernel(*refs):
    o_ref, w_ref, x_ref, g_ref, b_ref, y_ref = refs[-6:]
    h = jnp.dot(o_ref[...].astype(BF16), w_ref[...], preferred_element_type=F32)
    _post_norm_store(x_ref[...], h, g_ref, b_ref, y_ref)


def _outproj_postnorm(o, w_out, x, row0, tm, g, b, buf=None):
    nrows, kdim = o.shape
    d = w_out.shape[1]
    rb0 = row0 // tm
    out_spec, out_shape, alias_specs, alias_args = _row_out(tm, rb0, x.shape[0], d, buf)
    return pl.pallas_call(
        _outproj_kernel,
        grid=(nrows // tm,),
        in_specs=alias_specs + [
            pl.BlockSpec((tm, kdim), lambda i: (i, 0)),
            pl.BlockSpec((kdim, d), lambda i: (0, 0)),
            pl.BlockSpec((tm, d), lambda i: (rb0 + i, 0)),
            pl.BlockSpec((1, d), lambda i: (0, 0)),
            pl.BlockSpec((1, d), lambda i: (0, 0))],
        out_specs=out_spec,
        out_shape=out_shape,
        input_output_aliases={0: 0} if buf is not None else {},
        compiler_params=_cparams(("parallel",)),
    )(*alias_args, o, w_out, x, g, b)


def _pool_kernel(*refs, tm, tiles_per_seq, start_pos, zero_first):
    x_ref, halo_ref, w_ref, sc_ref, g_ref, b_ref, y_ref, xe_ref, h_ref = refs[-9:]
    i = pl.program_id(0) % tiles_per_seq
    halo = halo_ref[...].reshape(HALO, D_MODEL)
    if zero_first:
        halo = jnp.where(i == 0, 0.0, halo)
    x = x_ref[...]
    xe_ref[0:HALO, :] = halo
    xe_ref[HALO:HALO + tm, :] = x
    pos = start_pos + i * tm + lax.broadcasted_iota(I32, (tm, 1), 0)
    for gi, w in enumerate(POOL_WINDOWS):
        cols = slice(gi * POOL_GROUP, (gi + 1) * POOL_GROUP)
        win = x[:, cols]
        for j in range(1, w):
            win = win + xe_ref[HALO - j:HALO - j + tm, cols]
        count = jnp.minimum(w, pos + 1).astype(F32)
        mixed = win / count - x[:, cols]
        y = jnp.dot(mixed.astype(BF16), w_ref[gi], preferred_element_type=F32)
        h_ref[:, cols] = y * sc_ref[:, cols]
    _post_norm_store(x, h_ref[...], g_ref, b_ref, y_ref)


def _pool_postnorm(x, row0, nrows, tm, tiles_per_seq, halo, halo_map, start_pos, zero_first,
                   w_pool, scale, g, b, buf=None):
    rb0 = row0 // tm
    out_spec, out_shape, alias_specs, alias_args = _row_out(tm, rb0, x.shape[0], D_MODEL, buf)
    kern = functools.partial(_pool_kernel, tm=tm, tiles_per_seq=tiles_per_seq,
                             start_pos=start_pos, zero_first=zero_first)
    halo_block = (HALO, D_MODEL) if halo.ndim == 2 else (1, HALO, D_MODEL)
    return pl.pallas_call(
        kern,
        grid=(nrows // tm,),
        in_specs=alias_specs + [
            pl.BlockSpec((tm, D_MODEL), lambda i: (rb0 + i, 0)),
            pl.BlockSpec(halo_block, halo_map),
            pl.BlockSpec((N_POOL_GROUPS, POOL_GROUP, POOL_GROUP), lambda i: (0, 0, 0)),
            pl.BlockSpec((1, D_MODEL), lambda i: (0, 0)),
            pl.BlockSpec((1, D_MODEL), lambda i: (0, 0)),
            pl.BlockSpec((1, D_MODEL), lambda i: (0, 0))],
        out_specs=out_spec,
        out_shape=out_shape,
        input_output_aliases={0: 0} if buf is not None else {},
        scratch_shapes=[pltpu.VMEM((HALO + tm, D_MODEL), F32), pltpu.VMEM((tm, D_MODEL), F32)],
        compiler_params=_cparams(("parallel",)),
    )(*alias_args, x, halo, w_pool, scale, g, b)


def _split_bf16(a):
    hi = a.astype(BF16)
    lo = (a - hi.astype(F32)).astype(BF16)
    return hi, lo


def _router_kernel(*refs):
    x_ref, w_ref, b_ref, ti_ref, gk_ref = refs[-5:]
    xh, xl = _split_bf16(x_ref[...])
    wh, wl = _split_bf16(w_ref[...])
    logits = (jnp.dot(xh, wh, preferred_element_type=F32) + jnp.dot(xh, wl, preferred_element_type=F32)
              + jnp.dot(xl, wh, preferred_element_type=F32)) + b_ref[...]
    tm = logits.shape[0]
    lane = lax.broadcasted_iota(I32, (tm, N_EXPERTS), 1).astype(F32)
    slot = lax.broadcasted_iota(I32, (tm, TOP_K), 1)
    top_v = jnp.zeros((tm, TOP_K), F32)
    top_i = jnp.zeros((tm, TOP_K), F32)
    for k in range(TOP_K):
        m = jnp.max(logits, axis=-1, keepdims=True)
        idx = jnp.min(jnp.where(logits == m, lane, float(N_EXPERTS)), axis=-1, keepdims=True)
        top_v = jnp.where(slot == k, m, top_v)
        top_i = jnp.where(slot == k, idx, top_i)
        logits = jnp.where(lane == idx, NEG_INF, logits)
    e = jnp.exp(top_v - top_v[:, 0:1])
    gk_ref[...] = e / jnp.sum(e, axis=-1, keepdims=True)
    ti_ref[...] = top_i.astype(I32)


def _router(x, row0, nrows, tm, w_r, b_r, bufs=None):
    rb0 = row0 // tm
    n = x.shape[0]
    aliased = bufs is not None
    return pl.pallas_call(
        _router_kernel,
        grid=(nrows // tm,),
        in_specs=([pl.BlockSpec(memory_space=pl.ANY)] * 2 if aliased else []) + [
            pl.BlockSpec((tm, D_MODEL), lambda i: (rb0 + i, 0)),
            pl.BlockSpec((D_MODEL, N_EXPERTS), lambda i: (0, 0)),
            pl.BlockSpec((1, N_EXPERTS), lambda i: (0, 0))],
        out_specs=[pl.BlockSpec((tm, TOP_K), lambda i: (rb0 + i, 0))] * 2,
        out_shape=[jax.ShapeDtypeStruct((n, TOP_K), I32), jax.ShapeDtypeStruct((n, TOP_K), F32)],
        input_output_aliases={0: 0, 1: 1} if aliased else {},
        compiler_params=_cparams(("parallel",)),
    )(*(list(bufs) if aliased else []), x, w_r, b_r)


def _moe_plan(top_i):
    flat = top_i.reshape(N_ASSIGN)
    onehot = (flat[:, None] == jnp.arange(N_EXPERTS, dtype=I32)[None, :]).astype(I32)
    csum = jnp.cumsum(onehot, axis=0)
    rank = jnp.sum(csum * onehot, axis=1) - 1
    counts = csum[-1]
    tiles = (counts + MOE_TM - 1) // MOE_TM
    tile_end = jnp.cumsum(tiles)
    tile_start = tile_end - tiles
    dest = tile_start[flat] * MOE_TM + rank
    row_token = jnp.zeros((MOE_P,), I32).at[dest].set(jnp.arange(N_ASSIGN, dtype=I32) // TOP_K)
    groups = (tiles + MOE_RT - 1) // MOE_RT
    group_end = jnp.cumsum(groups)
    n_groups = group_end[-1]
    gid = jnp.arange(MOE_G, dtype=I32)
    g_eff = jnp.minimum(gid, n_groups - 1)
    g_exp = jnp.minimum(jnp.searchsorted(group_end, g_eff, side="right").astype(I32), N_EXPERTS - 1)
    local = g_eff - (group_end - groups)[g_exp]
    g_tile0 = tile_start[g_exp] + local * MOE_RT
    g_nt = jnp.where(gid < n_groups, jnp.clip(tiles[g_exp] - local * MOE_RT, 0, MOE_RT), 0)
    return (dest.astype(I32), row_token, g_exp, g_tile0.astype(I32), g_nt.astype(I32),
            n_groups.reshape(1).astype(I32))


def _moe_expert_kernel(gexp_ref, gtile0_ref, gnt_ref, ng_ref, tok_ref,
                       x_hbm, wg_ref, wu_ref, bg_ref, bu_ref, wd_ref, bd_ref, y_hbm,
                       xg_ref, acc_ref, gsem, ssem):
    g = pl.program_id(0)
    c = pl.program_id(1)
    last = pl.num_programs(1) - 1
    nt = gnt_ref[g]
    row0 = gtile0_ref[g] * MOE_TM
    slot = g & 1

    def gather(grp, slot, start):
        first = gtile0_ref[grp] * MOE_TM

        def row(r, _):
            cp = pltpu.make_async_copy(x_hbm.at[pl.ds(tok_ref[first + r], 1)],
                                       xg_ref.at[slot, pl.ds(r, 1)], gsem.at[slot])
            cp.start() if start else cp.wait()
            return 0

        lax.fori_loop(0, gnt_ref[grp] * MOE_TM, row, 0)

    @pl.when((c == 0) & (g == 0))
    def _first_rows():
        gather(g, slot, True)

    @pl.when((c == 0) & (nt > 0))
    def _rows():
        gather(g, slot, False)

        @pl.when(g + 1 < ng_ref[0])
        def _prefetch():
            gather(g + 1, 1 - slot, True)

    def tile_store(t):
        r0 = pl.multiple_of(t * MOE_TM, MOE_TM)
        return pltpu.make_async_copy(acc_ref.at[pl.ds(r0, MOE_TM)],
                                     y_hbm.at[pl.ds(pl.multiple_of(row0 + r0, MOE_TM), MOE_TM)], ssem)

    @pl.when(nt > 0)
    def _compute():
        wg = wg_ref[0, 0].astype(BF16)
        wu = wu_ref[0, 0].astype(BF16)
        wd = wd_ref[0, 0].astype(BF16)

        def tile(t, _):
            r0 = pl.multiple_of(t * MOE_TM, MOE_TM)
            xt = xg_ref[slot, pl.ds(r0, MOE_TM), :].astype(BF16)
            hg = jnp.dot(xt, wg, preferred_element_type=F32) + bg_ref[0, 0]
            hu = jnp.dot(xt, wu, preferred_element_type=F32) + bu_ref[0, 0]
            gate = jnp.minimum(hg, SWIGLU_LIMIT)
            lin = jnp.clip(hu, -SWIGLU_LIMIT, SWIGLU_LIMIT)
            act = gate * jax.nn.sigmoid(SWIGLU_ALPHA * gate) * (lin + 1.0)
            contrib = jnp.dot(act.astype(BF16), wd, preferred_element_type=F32)

            @pl.when(c == 0)
            def _first():
                acc_ref[pl.ds(r0, MOE_TM), :] = contrib + bd_ref[0, 0]

            @pl.when(c > 0)
            def _rest():
                acc_ref[pl.ds(r0, MOE_TM), :] += contrib

            @pl.when(c == last)
            def _store():
                tile_store(t).start()

            return 0

        lax.fori_loop(0, nt, tile, 0)

        @pl.when(c == last)
        def _stored():
            lax.fori_loop(0, nt, lambda t, _: (tile_store(t).wait(), 0)[1], 0)


def _moe_experts(x, plan, layer, w_gu, b_gu, w_d, b_d):
    _, row_token, g_exp, g_tile0, g_nt, n_groups = plan
    n_chunks = D_FF // MOE_F
    depth = w_gu.shape[0]

    def chunk(g, c, ng):
        return jnp.where(g < ng[0], c, n_chunks - 1)

    grid_spec = pltpu.PrefetchScalarGridSpec(
        num_scalar_prefetch=5,
        grid=(MOE_G, n_chunks),
        in_specs=[
            pl.BlockSpec(memory_space=pl.ANY),
            pl.BlockSpec((1, 1, D_MODEL, MOE_F),
                         lambda g, c, ge, gt, gn, ng, tok: (layer, ge[g], 0, chunk(g, c, ng))),
            pl.BlockSpec((1, 1, D_MODEL, MOE_F),
                         lambda g, c, ge, gt, gn, ng, tok: (layer, ge[g], 0, n_chunks + chunk(g, c, ng))),
            pl.BlockSpec((1, 1, 1, MOE_F), lambda g, c, ge, gt, gn, ng, tok: (layer, ge[g], 0, chunk(g, c, ng))),
            pl.BlockSpec((1, 1, 1, MOE_F),
                         lambda g, c, ge, gt, gn, ng, tok: (layer, ge[g], 0, n_chunks + chunk(g, c, ng))),
            pl.BlockSpec((1, 1, MOE_F, D_MODEL),
                         lambda g, c, ge, gt, gn, ng, tok: (layer, ge[g], chunk(g, c, ng), 0)),
            pl.BlockSpec((1, 1, 1, D_MODEL), lambda g, c, ge, gt, gn, ng, tok: (layer, ge[g], 0, 0)),
        ],
        out_specs=pl.BlockSpec(memory_space=pl.ANY),
        scratch_shapes=[pltpu.VMEM((2, MOE_RT * MOE_TM, D_MODEL), F32),
                        pltpu.VMEM((MOE_RT * MOE_TM, D_MODEL), F32),
                        pltpu.SemaphoreType.DMA((2,)), pltpu.SemaphoreType.DMA(())],
    )
    b_gu4 = b_gu.reshape(depth, N_EXPERTS, 1, 2 * D_FF)
    return pl.pallas_call(
        _moe_expert_kernel,
        grid_spec=grid_spec,
        out_shape=jax.ShapeDtypeStruct((MOE_P, D_MODEL), F32),
        compiler_params=_cparams(("arbitrary", "arbitrary")),
    )(g_exp, g_tile0, g_nt, n_groups, row_token, x, w_gu, w_gu, b_gu4, b_gu4,
      w_d, b_d.reshape(depth, N_EXPERTS, 1, D_MODEL))


def _moe_combine_kernel(dest_ref, *refs, tm, row0):
    y_hbm, gk_ref, x_ref, g_ref, b_ref, o_ref, ybuf_ref, sem = refs[-8:]
    base = (row0 + pl.program_id(0) * tm) * TOP_K

    def row_copy(a):
        r, k = lax.shift_right_logical(a, TOP_K_SHIFT), a & (TOP_K - 1)
        return pltpu.make_async_copy(y_hbm.at[pl.ds(dest_ref[base + a], 1)], ybuf_ref.at[k, pl.ds(r, 1)], sem)

    lax.fori_loop(0, tm * TOP_K, lambda a, _: (row_copy(a).start(), 0)[1], 0)
    lax.fori_loop(0, tm * TOP_K, lambda a, _: (row_copy(a).wait(), 0)[1], 0)
    gk = gk_ref[...]
    out = gk[:, 0:1] * ybuf_ref[0]
    for k in range(1, TOP_K):
        out = out + gk[:, k:k + 1] * ybuf_ref[k]
    _post_norm_store(x_ref[...], out, g_ref, b_ref, o_ref)


def _moe_combine_postnorm(y_sorted, dest, gk, x, row0, nrows, tm, g, b, buf=None):
    rb0 = row0 // tm
    out_spec, out_shape, alias_specs, alias_args = _row_out(tm, rb0, x.shape[0], D_MODEL, buf)
    kern = functools.partial(_moe_combine_kernel, tm=tm, row0=row0)
    grid_spec = pltpu.PrefetchScalarGridSpec(
        num_scalar_prefetch=1,
        grid=(nrows // tm,),
        in_specs=alias_specs + [
            pl.BlockSpec(memory_space=pl.ANY),
            pl.BlockSpec((tm, TOP_K), lambda i, d: (rb0 + i, 0)),
            pl.BlockSpec((tm, D_MODEL), lambda i, d: (rb0 + i, 0)),
            pl.BlockSpec((1, D_MODEL), lambda i, d: (0, 0)),
            pl.BlockSpec((1, D_MODEL), lambda i, d: (0, 0))],
        out_specs=out_spec,
        scratch_shapes=[pltpu.VMEM((TOP_K, tm, D_MODEL), F32), pltpu.SemaphoreType.DMA(())],
    )
    return pl.pallas_call(
        kern,
        grid_spec=grid_spec,
        out_shape=out_shape,
        input_output_aliases={1: 0} if buf is not None else {},
        compiler_params=_cparams(("arbitrary",)),
    )(dest, *alias_args, y_sorted, gk, x, g, b)


def _moe_postnorm(x, layer, w_r, b_r, w_gu, b_gu, w_d, b_d, g, b):
    routed = _router(x, 0, N_PROMPT, TM_PROMPT, w_r, b_r)
    top_i, gk = _router(x, N_PROMPT, N_SAMPLE, TM_SAMPLE, w_r, b_r, bufs=routed)
    plan = _moe_plan(top_i)
    y_sorted = _moe_experts(x, plan, layer, w_gu, b_gu, w_d, b_d)
    buf = _moe_combine_postnorm(y_sorted, plan[0], gk, x, 0, N_PROMPT, TM_PROMPT, g, b)
    return _moe_combine_postnorm(y_sorted, plan[0], gk, x, N_PROMPT, N_SAMPLE, TM_SAMPLE, g, b, buf=buf)


def _attn_layer(x, layer, cache_k, cache_v, cache_kidx, page_table, w_in, kn_g, kn_b, w_out, g, b):
    w_in_bf = w_in.astype(BF16)
    w_out_bf = w_out.astype(BF16)
    w_kiwi = jnp.pad(w_in_bf[:, OFF_KI:], ((0, 0), (0, 2 * LANES - (IN_W - OFF_KI))))
    kn_g = kn_g.reshape(1, IDX_DIM)
    kn_b = kn_b.reshape(1, IDX_DIM)

    def project(row0, nrows, tm, q_dtype):
        tm_mm = min(TM_MM, nrows)
        (q,) = _mm(x, row0, nrows, tm_mm, w_in_bf, 0, ATTN_W, (q_dtype,))
        k, kbf = _mm(x, row0, nrows, tm_mm, w_in_bf, OFF_K, ATTN_W, (F32, BF16))
        v, vbf = _mm(x, row0, nrows, tm_mm, w_in_bf, OFF_V, ATTN_W, (F32, BF16))
        (qi,) = _mm(x, row0, nrows, tm_mm, w_in_bf, OFF_QI, IDX_W, (BF16,))
        ki, wi = _kiwi(x, row0, nrows, tm, w_kiwi, kn_g, kn_b)
        return q, k, kbf, v, vbf, qi, ki, wi

    q, k_p, kbf, v_p, vbf, qi, ki_p, wi = project(0, N_PROMPT, TM_PROMPT, BF16)
    s3 = lambda a: a.reshape(BATCH, SEQ, a.shape[-1])
    o_p = _dsa_prompt(s3(q), s3(qi), s3(wi), s3(ki_p), s3(kbf), s3(vbf)).reshape(N_PROMPT, ATTN_W)

    q, k_s, _, v_s, _, qi, ki_s, wi = project(N_PROMPT, N_SAMPLE, TM_SAMPLE, F32)
    ki_new_pad = jnp.pad(ki_s.reshape(DEC_BATCH, DEC_SEQ, IDX_DIM), ((0, 0), (0, PAGE_SIZE - DEC_SEQ), (0, 0)))
    idx, cnt, n_sel = _dsa_sample_select(page_table, layer,
                                         qi.reshape(DEC_BATCH, DEC_SEQ * N_IDX_HEADS, IDX_DIM),
                                         wi.reshape(DEC_BATCH, DEC_SEQ * N_IDX_HEADS, 1),
                                         cache_kidx, ki_new_pad)
    heads = lambda a: a.reshape(N_SAMPLE, N_HEADS, HEAD_DIM)
    o_s = _dsa_sample_attend(idx, cnt, n_sel, page_table, layer, heads(q), cache_k, cache_v,
                             heads(k_s), heads(v_s)).reshape(N_SAMPLE, ATTN_W)

    buf = _outproj_postnorm(o_p, w_out_bf, x, 0, TM_PROMPT, g, b)
    x1 = _outproj_postnorm(o_s, w_out_bf, x, N_PROMPT, TM_SAMPLE, g, b, buf=buf)
    return x1, (k_p, v_p, ki_p, k_s, v_s, ki_s)


def _pool_layer(x, state, w_pool, scale, g, b):
    w_bf = w_pool.astype(BF16)
    scale = scale.reshape(1, D_MODEL)
    tiles = SEQ // TM_PROMPT
    hb = TM_PROMPT // HALO
    buf = _pool_postnorm(x, 0, N_PROMPT, TM_PROMPT, tiles, x,
                         lambda i: (jnp.maximum(i * hb - 1, 0), 0), 0, True, w_bf, scale, g, b)
    hist = jnp.pad(state, ((0, 0), (HALO - POOL_HIST, 0), (0, 0)))
    x1 = _pool_postnorm(x, N_PROMPT, N_SAMPLE, DEC_SEQ, 1, hist,
                        lambda i: (i, 0, 0), PAST_LEN, False, w_bf, scale, g, b, buf=buf)
    xp = x[:N_PROMPT].reshape(BATCH, SEQ, D_MODEL)
    xs = x[N_PROMPT:].reshape(DEC_BATCH, DEC_SEQ, D_MODEL)
    new_state_p = xp[:, SEQ - POOL_HIST:]
    new_state_s = jnp.concatenate([state, xs], axis=1)[:, -POOL_HIST:]
    return x1, (new_state_p, new_state_s)


def kernel(x_prompt, x_sample, cache_k, cache_v, cache_kidx, state_pool, page_table, w_attn_in, idx_knorm_g, idx_knorm_b, w_attn_out, w_pool, pool_scale, ln_g, ln_b, w_router, b_router, w_gate_up, b_gate_up, w_down, b_down):
    x = jnp.concatenate([x_prompt.reshape(N_PROMPT, D_MODEL), x_sample.reshape(N_SAMPLE, D_MODEL)], axis=0)
    attn_rows, pool_rows = [], []
    for i in range(DEPTH):
        j = i // N_MIXERS
        g0, b0 = ln_g[i, 0].reshape(1, D_MODEL), ln_b[i, 0].reshape(1, D_MODEL)
        g1, b1 = ln_g[i, 1].reshape(1, D_MODEL), ln_b[i, 1].reshape(1, D_MODEL)
        if i % N_MIXERS == 0:
            x, rows = _attn_layer(x, j, cache_k, cache_v, cache_kidx, page_table,
                                  w_attn_in[j], idx_knorm_g[j], idx_knorm_b[j], w_attn_out[j], g0, b0)
            attn_rows.append(rows)
        else:
            x, rows = _pool_layer(x, state_pool[j], w_pool[j], pool_scale[j], g0, b0)
            pool_rows.append(rows)
        x = _moe_postnorm(x, i, w_router[i], b_router[i].reshape(1, N_EXPERTS), w_gate_up, b_gate_up,
                          w_down, b_down, g1, b1)

    def stack(rows, idx, lead, tail):
        return jnp.stack([r[idx].reshape(lead + tail) for r in rows])

    hd = (N_HEADS, HEAD_DIM)
    return (x[:N_PROMPT].reshape(BATCH, SEQ, D_MODEL),
            x[N_PROMPT:].reshape(DEC_BATCH, DEC_SEQ, D_MODEL),
            stack(attn_rows, 0, (BATCH, SEQ), hd), stack(attn_rows, 1, (BATCH, SEQ), hd),
            stack(attn_rows, 2, (BATCH, SEQ), (IDX_DIM,)),
            jnp.stack([r[0] for r in pool_rows]),
            stack(attn_rows, 3, (DEC_BATCH, DEC_SEQ), hd), stack(attn_rows, 4, (DEC_BATCH, DEC_SEQ), hd),
            stack(attn_rows, 5, (DEC_BATCH, DEC_SEQ), (IDX_DIM,)),
            jnp.stack([r[1] for r in pool_rows]))
```

```python
import functools

import jax
import jax.numpy as jnp
from jax import lax
from jax.experimental import pallas as pl
from jax.experimental.pallas import tpu as pltpu

D_MODEL = 2048
BATCH = 4
SEQ = 2048
DEPTH = 4
DEC_BATCH = 8
DEC_SEQ = 8
PAST_LEN = 16384
PAGE_SIZE = 128

N_MIXERS = 2
N_HEADS = 16
HEAD_DIM = 128
ATTN_W = N_HEADS * HEAD_DIM
N_IDX_HEADS = 16
IDX_DIM = 128
IDX_W = N_IDX_HEADS * IDX_DIM
TOPK_MAX = 256
ATTN_SCALE = HEAD_DIM ** -0.5
IDX_SCALE = IDX_DIM ** -0.5
OFF_K = ATTN_W
OFF_V = 2 * ATTN_W
OFF_QI = 3 * ATTN_W
OFF_KI = OFF_QI + IDX_W
OFF_WI = OFF_KI + IDX_DIM
IN_W = OFF_WI + N_IDX_HEADS
POOL_WINDOWS = (2, 4, 8, 16)
N_POOL_GROUPS = len(POOL_WINDOWS)
POOL_GROUP = D_MODEL // N_POOL_GROUPS
POOL_HIST = max(POOL_WINDOWS) - 1
N_EXPERTS = 32
TOP_K = 4
D_FF = D_MODEL
SWIGLU_LIMIT = 7.0
SWIGLU_ALPHA = 1.702
LN_EPS = 1e-5
DEEPNORM_ALPHA = (2 * DEPTH) ** 0.25

N_PROMPT = BATCH * SEQ
N_SAMPLE = DEC_BATCH * DEC_SEQ
N_TOK = N_PROMPT + N_SAMPLE
N_PAGES = PAST_LEN // PAGE_SIZE

LANES = 128
HALO = 16
VMEM_LIMIT = 56 * 1024 * 1024

TM_PROMPT = min(256, N_PROMPT)
TM_SAMPLE = N_SAMPLE
TM_MM = min(1024, N_PROMPT)
TN_MM = min(512, ATTN_W)
TQ = min(256, SEQ)
SEL_G = min(8, N_PAGES)
MOE_TM = min(224, N_PROMPT)
MOE_RT = 5
MOE_F = min(256, D_FF)
DMA_UNROLL = 8
TOP_K_SHIFT = TOP_K.bit_length() - 1
PAGE_SHIFT = PAGE_SIZE.bit_length() - 1
N_ASSIGN = N_TOK * TOP_K
MOE_P = (N_ASSIGN // MOE_TM + N_EXPERTS) * MOE_TM
MOE_G = (N_ASSIGN // MOE_TM + N_EXPERTS) // MOE_RT + N_EXPERTS + 1

F32 = jnp.float32
BF16 = jnp.bfloat16
I32 = jnp.int32
NEG_INF = float("-inf")
INT_MIN = -2 ** 31

assert 1 << TOP_K_SHIFT == TOP_K and 1 << PAGE_SHIFT == PAGE_SIZE and N_PAGES % SEL_G == 0
assert MOE_TM % DMA_UNROLL == 0


def _cparams(sem):
    return pltpu.CompilerParams(dimension_semantics=sem, vmem_limit_bytes=VMEM_LIMIT)


def _layer_norm(x, g, b):
    mu = jnp.mean(x, axis=-1, keepdims=True)
    xc = x - mu
    var = jnp.mean(xc * xc, axis=-1, keepdims=True)
    return xc * lax.rsqrt(var + LN_EPS) * g + b


def _post_norm_store(x, h, g_ref, b_ref, o_ref):
    o_ref[...] = _layer_norm(DEEPNORM_ALPHA * x + h, g_ref[...], b_ref[...])


def _mm_kernel(x_ref, w_ref, *o_refs):
    acc = jnp.dot(x_ref[...].astype(BF16), w_ref[...], preferred_element_type=F32)
    for o in o_refs:
        o[...] = acc.astype(o.dtype)


def _mm(x, row0, nrows, tm, w, col0, ncols, out_dtypes):
    k = x.shape[1]
    tn = min(TN_MM, ncols)
    rb0, cb0 = row0 // tm, col0 // tn
    return pl.pallas_call(
        _mm_kernel,
        grid=(nrows // tm, ncols // tn),
        in_specs=[pl.BlockSpec((tm, k), lambda i, j: (rb0 + i, 0)),
                  pl.BlockSpec((k, tn), lambda i, j: (0, cb0 + j))],
        out_specs=[pl.BlockSpec((tm, tn), lambda i, j: (i, j)) for _ in out_dtypes],
        out_shape=[jax.ShapeDtypeStruct((nrows, ncols), dt) for dt in out_dtypes],
        compiler_params=_cparams(("parallel", "arbitrary")),
    )(x, w)


def _kiwi_kernel(x_ref, w_ref, g_ref, b_ref, ki_ref, wi_ref):
    p = jnp.dot(x_ref[...].astype(BF16), w_ref[...], preferred_element_type=F32)
    ki_ref[...] = _layer_norm(p[:, :IDX_DIM], g_ref[...], b_ref[...])
    wi_ref[...] = p[:, IDX_DIM:IDX_DIM + N_IDX_HEADS] * (N_IDX_HEADS ** -0.5)


def _kiwi(x, row0, nrows, tm, w_kiwi, g, b):
    k = x.shape[1]
    rb0 = row0 // tm
    wcols = w_kiwi.shape[1]
    return pl.pallas_call(
        _kiwi_kernel,
        grid=(nrows // tm,),
        in_specs=[pl.BlockSpec((tm, k), lambda i: (rb0 + i, 0)),
                  pl.BlockSpec((k, wcols), lambda i: (0, 0)),
                  pl.BlockSpec((1, IDX_DIM), lambda i: (0, 0)),
                  pl.BlockSpec((1, IDX_DIM), lambda i: (0, 0))],
        out_specs=[pl.BlockSpec((tm, IDX_DIM), lambda i: (i, 0)),
                   pl.BlockSpec((tm, N_IDX_HEADS), lambda i: (i, 0))],
        out_shape=[jax.ShapeDtypeStruct((nrows, IDX_DIM), F32),
                   jax.ShapeDtypeStruct((nrows, N_IDX_HEADS), F32)],
        compiler_params=_cparams(("parallel",)),
    )(x, w_kiwi, g, b)


def _count(mask):
    return jnp.sum(jnp.where(mask, 1.0, 0.0), axis=-1, keepdims=True)


def _kth_largest_key(key, k):
    lo = jnp.where(_count(key >= 0) >= k, 0, INT_MIN).astype(I32)

    def body(i, lo):
        cand = lo + jnp.left_shift(jnp.int32(1), 30 - i)
        return jnp.where(_count(key >= cand) >= k, cand, lo)

    return lax.fori_loop(0, 31, body, lo)


def _strictly_lower():
    before = lax.broadcasted_iota(I32, (LANES, LANES), 0) < lax.broadcasted_iota(I32, (LANES, LANES), 1)
    return jnp.where(before, 1.0, 0.0).astype(BF16)


def _select_bias(score, adm, k, key_ref, bias_ref):
    score = jnp.where(score == 0.0, 0.0, score)
    bits = pltpu.bitcast(score, I32)
    key = jnp.where(bits < 0, bits ^ jnp.int32(0x7FFFFFFF), bits)
    key = jnp.where(adm, key, INT_MIN)
    thr = jnp.maximum(_kth_largest_key(key, k), INT_MIN + 1)
    ge = key >= thr
    bias_ref[...] = jnp.where(ge, 0.0, NEG_INF)
    n_ge = _count(ge)

    @pl.when(jnp.max(n_ge) > k)
    def _ties():
        key_ref[...] = key
        need = k - _count(key > thr)
        tri = _strictly_lower()

        def tile(j, carry):
            off = pl.multiple_of(j * LANES, LANES)
            kj = key_ref[:, pl.ds(off, LANES)]
            eq = kj == thr
            e = jnp.where(eq, 1.0, 0.0)
            rank = jnp.dot(e.astype(BF16), tri, preferred_element_type=F32) + carry
            sel = (kj > thr) | (eq & (rank < need))
            bias_ref[:, pl.ds(off, LANES)] = jnp.where(sel, 0.0, NEG_INF)
            return carry + jnp.sum(e, axis=-1, keepdims=True)

        lax.fori_loop(0, key_ref.shape[1] // LANES, tile, jnp.zeros_like(need))


def _dsa_prompt_kernel(q_ref, qi_ref, wi_ref, ki_ref, k_ref, v_ref, o_ref, key_ref, bias_ref, *, n_sel):
    i = pl.program_id(1)
    s_len = ki_ref.shape[1]
    ki = ki_ref[0].astype(BF16)
    wi = wi_ref[0]
    score = jnp.zeros((TQ, s_len), F32)
    for h in range(N_IDX_HEADS):
        d = lax.dot_general(qi_ref[0, :, h * IDX_DIM:(h + 1) * IDX_DIM], ki,
                            (((1,), (1,)), ((), ())), preferred_element_type=F32)
        score = score + jnp.maximum(d, 0.0) * wi[:, h:h + 1]
    score = score * IDX_SCALE
    t_pos = i * TQ + lax.broadcasted_iota(I32, (TQ, s_len), 0)
    adm = lax.broadcasted_iota(I32, (TQ, s_len), 1) <= t_pos
    _select_bias(jnp.where(adm, score, NEG_INF), adm, n_sel, key_ref, bias_ref)

    for h in range(N_HEADS):
        cols = slice(h * HEAD_DIM, (h + 1) * HEAD_DIM)
        logits = lax.dot_general(q_ref[0, :, cols], k_ref[0, :, cols],
                                 (((1,), (1,)), ((), ())), preferred_element_type=F32)
        logits = logits * ATTN_SCALE + bias_ref[...]
        m = jnp.max(logits, axis=-1, keepdims=True)
        p = jnp.exp(logits - m)
        denom = jnp.sum(p, axis=-1, keepdims=True)
        o = jnp.dot(p.astype(BF16), v_ref[0, :, cols], preferred_element_type=F32)
        o_ref[0, :, cols] = (o / denom).astype(o_ref.dtype)


def _dsa_prompt(q, qi, wi, ki, k, v):
    b, t = q.shape[:2]
    n_sel = min(TOPK_MAX, t // 4)
    kern = functools.partial(_dsa_prompt_kernel, n_sel=n_sel)
    once = pl.Buffered(1)
    return pl.pallas_call(
        kern,
        grid=(b, t // TQ),
        in_specs=[pl.BlockSpec((1, TQ, ATTN_W), lambda bb, i: (bb, i, 0)),
                  pl.BlockSpec((1, TQ, IDX_W), lambda bb, i: (bb, i, 0)),
                  pl.BlockSpec((1, TQ, N_IDX_HEADS), lambda bb, i: (bb, i, 0)),
                  pl.BlockSpec((1, t, IDX_DIM), lambda bb, i: (bb, 0, 0), pipeline_mode=once),
                  pl.BlockSpec((1, t, ATTN_W), lambda bb, i: (bb, 0, 0), pipeline_mode=once),
                  pl.BlockSpec((1, t, ATTN_W), lambda bb, i: (bb, 0, 0), pipeline_mode=once)],
        out_specs=pl.BlockSpec((1, TQ, ATTN_W), lambda bb, i: (bb, i, 0)),
        out_shape=jax.ShapeDtypeStruct((b, t, ATTN_W), BF16),
        scratch_shapes=[pltpu.VMEM((TQ, t), I32), pltpu.VMEM((TQ, t), F32)],
        compiler_params=_cparams(("parallel", "arbitrary")),
    )(q, qi, wi, ki, k, v)


def _dsa_sample_select_kernel(pt_ref, qi_ref, wi_ref, *refs, n_sel):
    cki_refs = refs[:SEL_G]
    kin_ref, idx_ref, cnt_ref, score_ref, key_ref, bias_ref, hit_ref = refs[SEL_G:]
    p = pl.program_id(1)
    tn = DEC_SEQ
    n_steps = N_PAGES // SEL_G
    s_len = score_ref.shape[1]

    def page_score(kpage, off):
        d = lax.dot_general(qi_ref[0], kpage.astype(BF16), (((1,), (1,)), ((), ())), preferred_element_type=F32)
        r = jnp.maximum(d, 0.0) * wi_ref[0]
        score_ref[:, pl.ds(off, PAGE_SIZE)] = jnp.sum(r.reshape(tn, N_IDX_HEADS, PAGE_SIZE), axis=1) * IDX_SCALE

    @pl.when(p < n_steps)
    def _cached_pages():
        for gi in range(SEL_G):
            page_score(cki_refs[gi][0, 0], pl.multiple_of((p * SEL_G + gi) * PAGE_SIZE, PAGE_SIZE))

    @pl.when(p == n_steps)
    def _new_keys_then_select():
        page_score(kin_ref[0], PAST_LEN)
        q_pos = PAST_LEN + lax.broadcasted_iota(I32, (tn, s_len), 0)
        adm = lax.broadcasted_iota(I32, (tn, s_len), 1) <= q_pos
        _select_bias(jnp.where(adm, score_ref[...], NEG_INF), adm, n_sel, key_ref, bias_ref)

        hit_ref[...] = jnp.zeros_like(hit_ref)
        tri = _strictly_lower()
        slot = lax.broadcasted_iota(I32, (n_sel, LANES), 0).astype(F32)
        lane = lax.broadcasted_iota(I32, (1, LANES), 1)

        def tile(t, carry):
            off = pl.multiple_of(t * LANES, LANES)
            e = jnp.where(bias_ref[:, pl.ds(off, LANES)] == 0.0, 1.0, 0.0)
            rank = jnp.dot(e.astype(BF16), tri, preferred_element_type=F32) + carry
            pos = (lane + off).astype(F32)
            for q in range(tn):
                hit = (rank[q:q + 1, :] == slot) & (e[q:q + 1, :] > 0.0)
                hit_ref[q] += jnp.where(hit, pos, 0.0)
            return carry + jnp.sum(e, axis=-1, keepdims=True)

        cnt = lax.fori_loop(0, s_len // LANES, tile, jnp.zeros((tn, 1), F32))
        for q in range(tn):
            idx_ref[0, q * n_sel:(q + 1) * n_sel, :] = jnp.sum(hit_ref[q], axis=-1, keepdims=True).astype(I32)
        cnt_ref[0] = cnt.astype(I32)


def _dsa_sample_select(page_table, layer, qi_flat, wi_flat, cki, ki_new_pad):
    s_len = PAST_LEN + PAGE_SIZE
    n_sel = min(TOPK_MAX, (PAST_LEN + DEC_SEQ) // 4)
    rows = DEC_SEQ * N_IDX_HEADS
    n_steps = N_PAGES // SEL_G
    kern = functools.partial(_dsa_sample_select_kernel, n_sel=n_sel)

    def page_spec(gi):
        def index(b, p, pt):
            return (layer, pt[b * N_PAGES + jnp.minimum(p, n_steps - 1) * SEL_G + gi], 0, 0)
        return pl.BlockSpec((1, 1, PAGE_SIZE, IDX_DIM), index)

    grid_spec = pltpu.PrefetchScalarGridSpec(
        num_scalar_prefetch=1,
        grid=(DEC_BATCH, n_steps + 1),
        in_specs=[pl.BlockSpec((1, rows, IDX_DIM), lambda b, p, pt: (b, 0, 0)),
                  pl.BlockSpec((1, rows, 1), lambda b, p, pt: (b, 0, 0))]
                 + [page_spec(gi) for gi in range(SEL_G)]
                 + [pl.BlockSpec((1, PAGE_SIZE, IDX_DIM), lambda b, p, pt: (b, 0, 0))],
        out_specs=[pl.BlockSpec((1, DEC_SEQ * n_sel, 1), lambda b, p, pt: (b, 0, 0)),
                   pl.BlockSpec((1, DEC_SEQ, 1), lambda b, p, pt: (b, 0, 0))],
        scratch_shapes=[pltpu.VMEM((DEC_SEQ, s_len), F32), pltpu.VMEM((DEC_SEQ, s_len), I32),
                        pltpu.VMEM((DEC_SEQ, s_len), F32), pltpu.VMEM((DEC_SEQ, n_sel, LANES), F32)],
    )
    idx, cnt = pl.pallas_call(
        kern,
        grid_spec=grid_spec,
        out_shape=[jax.ShapeDtypeStruct((DEC_BATCH, DEC_SEQ * n_sel, 1), I32),
                   jax.ShapeDtypeStruct((DEC_BATCH, DEC_SEQ, 1), I32)],
        compiler_params=_cparams(("parallel", "arbitrary")),
    )(page_table.reshape(-1), qi_flat, wi_flat, *([cki] * SEL_G), ki_new_pad)
    return idx.reshape(-1), cnt.reshape(-1), n_sel


def _dsa_sample_attend_kernel(idx_ref, cnt_ref, pt_ref, q_ref, ck_hbm, cv_hbm, kn_hbm, vn_hbm, o_ref,
                              kbuf, vbuf, ksem, vsem, *, layer, n_sel):
    i = pl.program_id(0)
    slot = i & 1

    def fetch(step, slot):
        b = lax.div(step, DEC_SEQ)

        def row(j, _):
            s = idx_ref[step * n_sel + j]

            @pl.when(s < PAST_LEN)
            def _cached():
                page = pt_ref[b * N_PAGES + lax.shift_right_logical(s, PAGE_SHIFT)]
                r = s & (PAGE_SIZE - 1)
                pltpu.make_async_copy(ck_hbm.at[layer, page, r], kbuf.at[slot, j], ksem.at[slot]).start()
                pltpu.make_async_copy(cv_hbm.at[layer, page, r], vbuf.at[slot, j], vsem.at[slot]).start()

            @pl.when(s >= PAST_LEN)
            def _new():
                n = b * DEC_SEQ + (s - PAST_LEN)
                pltpu.make_async_copy(kn_hbm.at[n], kbuf.at[slot, j], ksem.at[slot]).start()
                pltpu.make_async_copy(vn_hbm.at[n], vbuf.at[slot, j], vsem.at[slot]).start()

            return 0

        lax.fori_loop(0, n_sel, row, 0)

    def wait(slot):
        def row(j, _):
            pltpu.make_async_copy(kn_hbm.at[0], kbuf.at[slot, j], ksem.at[slot]).wait()
            pltpu.make_async_copy(vn_hbm.at[0], vbuf.at[slot, j], vsem.at[slot]).wait()
            return 0

        lax.fori_loop(0, n_sel, row, 0, unroll=DMA_UNROLL)

    @pl.when(i == 0)
    def _first():
        fetch(i, slot)

    wait(slot)

    @pl.when(i + 1 < pl.num_programs(0))
    def _prefetch():
        fetch(i + 1, 1 - slot)

    k = kbuf[slot]
    logits = jnp.sum(k * q_ref[...], axis=-1, keepdims=True) * ATTN_SCALE
    valid = lax.broadcasted_iota(I32, logits.shape, 0) < cnt_ref[i]
    logits = jnp.where(valid, logits, NEG_INF)
    m = jnp.max(logits, axis=0, keepdims=True)
    p = jnp.exp(logits - m)
    denom = jnp.sum(p, axis=0, keepdims=True)
    o_ref[...] = jnp.sum(p * vbuf[slot], axis=0, keepdims=True) / denom


def _dsa_sample_attend(idx, cnt, n_sel, page_table, layer, q, ck, cv, k_new, v_new):
    n = q.shape[0]
    grid_spec = pltpu.PrefetchScalarGridSpec(
        num_scalar_prefetch=3,
        grid=(n,),
        in_specs=[pl.BlockSpec((1, N_HEADS, HEAD_DIM), lambda i, *_: (i, 0, 0))]
                 + [pl.BlockSpec(memory_space=pl.ANY)] * 4,
        out_specs=pl.BlockSpec((1, N_HEADS, HEAD_DIM), lambda i, *_: (i, 0, 0)),
        scratch_shapes=[pltpu.VMEM((2, n_sel, N_HEADS, HEAD_DIM), F32),
                        pltpu.VMEM((2, n_sel, N_HEADS, HEAD_DIM), F32),
                        pltpu.SemaphoreType.DMA((2,)), pltpu.SemaphoreType.DMA((2,))],
    )
    return pl.pallas_call(
        functools.partial(_dsa_sample_attend_kernel, layer=layer, n_sel=n_sel),
        grid_spec=grid_spec,
        out_shape=jax.ShapeDtypeStruct((n, N_HEADS, HEAD_DIM), F32),
        compiler_params=_cparams(("arbitrary",)),
    )(idx, cnt, page_table.reshape(-1), q, ck, cv, k_new, v_new)


def _row_out(tm, rb0, n_tok_rows, width, buf):
    spec = pl.BlockSpec((tm, width), lambda i, *_: (rb0 + i, 0))
    shape = jax.ShapeDtypeStruct((n_tok_rows, width), F32)
    alias_specs = [pl.BlockSpec(memory_space=pl.ANY)] if buf is not None else []
    alias_args = [buf] if buf is not None else []
    return spec, shape, alias_specs, alias_args


def _outproj_kernel(*refs):
    o_ref, w_ref, x_ref, g_ref, b_ref, y_ref = refs[-6:]
    h = jnp.dot(o_ref[...].astype(BF16), w_ref[...], preferred_element_type=F32)
    _post_norm_store(x_ref[...], h, g_ref, b_ref, y_ref)


def _outproj_postnorm(o, w_out, x, row0, tm, g, b, buf=None):
    nrows, kdim = o.shape
    d = w_out.shape[1]
    rb0 = row0 // tm
    out_spec, out_shape, alias_specs, alias_args = _row_out(tm, rb0, x.shape[0], d, buf)
    return pl.pallas_call(
        _outproj_kernel,
        grid=(nrows // tm,),
        in_specs=alias_specs + [
            pl.BlockSpec((tm, kdim), lambda i: (i, 0)),
            pl.BlockSpec((kdim, d), lambda i: (0, 0)),
            pl.BlockSpec((tm, d), lambda i: (rb0 + i, 0)),
            pl.BlockSpec((1, d), lambda i: (0, 0)),
            pl.BlockSpec((1, d), lambda i: (0, 0))],
        out_specs=out_spec,
        out_shape=out_shape,
        input_output_aliases={0: 0} if buf is not None else {},
        compiler_params=_cparams(("parallel",)),
    )(*alias_args, o, w_out, x, g, b)


def _pool_kernel(*refs, tm, tiles_per_seq, start_pos, zero_first):
    x_ref, halo_ref, w_ref, sc_ref, g_ref, b_ref, y_ref, xe_ref, h_ref = refs[-9:]
    i = pl.program_id(0) % tiles_per_seq
    halo = halo_ref[...].reshape(HALO, D_MODEL)
    if zero_first:
        halo = jnp.where(i == 0, 0.0, halo)
    x = x_ref[...]
    xe_ref[0:HALO, :] = halo
    xe_ref[HALO:HALO + tm, :] = x
    pos = start_pos + i * tm + lax.broadcasted_iota(I32, (tm, 1), 0)
    for gi, w in enumerate(POOL_WINDOWS):
        cols = slice(gi * POOL_GROUP, (gi + 1) * POOL_GROUP)
        win = x[:, cols]
        for j in range(1, w):
            win = win + xe_ref[HALO - j:HALO - j + tm, cols]
        count = jnp.minimum(w, pos + 1).astype(F32)
        mixed = win / count - x[:, cols]
        y = jnp.dot(mixed.astype(BF16), w_ref[gi], preferred_element_type=F32)
        h_ref[:, cols] = y * sc_ref[:, cols]
    _post_norm_store(x, h_ref[...], g_ref, b_ref, y_ref)


def _pool_postnorm(x, row0, nrows, tm, tiles_per_seq, halo, halo_map, start_pos, zero_first,
                   w_pool, scale, g, b, buf=None):
    rb0 = row0 // tm
    out_spec, out_shape, alias_specs, alias_args = _row_out(tm, rb0, x.shape[0], D_MODEL, buf)
    kern = functools.partial(_pool_kernel, tm=tm, tiles_per_seq=tiles_per_seq,
                             start_pos=start_pos, zero_first=zero_first)
    halo_block = (HALO, D_MODEL) if halo.ndim == 2 else (1, HALO, D_MODEL)
    return pl.pallas_call(
        kern,
        grid=(nrows // tm,),
        in_specs=alias_specs + [
            pl.BlockSpec((tm, D_MODEL), lambda i: (rb0 + i, 0)),
            pl.BlockSpec(halo_block, halo_map),
            pl.BlockSpec((N_POOL_GROUPS, POOL_GROUP, POOL_GROUP), lambda i: (0, 0, 0)),
            pl.BlockSpec((1, D_MODEL), lambda i: (0, 0)),
            pl.BlockSpec((1, D_MODEL), lambda i: (0, 0)),
            pl.BlockSpec((1, D_MODEL), lambda i: (0, 0))],
        out_specs=out_spec,
        out_shape=out_shape,
        input_output_aliases={0: 0} if buf is not None else {},
        scratch_shapes=[pltpu.VMEM((HALO + tm, D_MODEL), F32), pltpu.VMEM((tm, D_MODEL), F32)],
        compiler_params=_cparams(("parallel",)),
    )(*alias_args, x, halo, w_pool, scale, g, b)


def _split_bf16(a):
    hi = a.astype(BF16)
    lo = (a - hi.astype(F32)).astype(BF16)
    return hi, lo


def _router_kernel(*refs):
    x_ref, w_ref, b_ref, ti_ref, gk_ref = refs[-5:]
    xh, xl = _split_bf16(x_ref[...])
    wh, wl = _split_bf16(w_ref[...])
    logits = (jnp.dot(xh, wh, preferred_element_type=F32) + jnp.dot(xh, wl, preferred_element_type=F32)
              + jnp.dot(xl, wh, preferred_element_type=F32)) + b_ref[...]
    tm = logits.shape[0]
    lane = lax.broadcasted_iota(I32, (tm, N_EXPERTS), 1).astype(F32)
    slot = lax.broadcasted_iota(I32, (tm, TOP_K), 1)
    top_v = jnp.zeros((tm, TOP_K), F32)
    top_i = jnp.zeros((tm, TOP_K), F32)
    for k in range(TOP_K):
        m = jnp.max(logits, axis=-1, keepdims=True)
        idx = jnp.min(jnp.where(logits == m, lane, float(N_EXPERTS)), axis=-1, keepdims=True)
        top_v = jnp.where(slot == k, m, top_v)
        top_i = jnp.where(slot == k, idx, top_i)
        logits = jnp.where(lane == idx, NEG_INF, logits)
    e = jnp.exp(top_v - top_v[:, 0:1])
    gk_ref[...] = e / jnp.sum(e, axis=-1, keepdims=True)
    ti_ref[...] = top_i.astype(I32)


def _router(x, row0, nrows, tm, w_r, b_r, bufs=None):
    rb0 = row0 // tm
    n = x.shape[0]
    aliased = bufs is not None
    return pl.pallas_call(
        _router_kernel,
        grid=(nrows // tm,),
        in_specs=([pl.BlockSpec(memory_space=pl.ANY)] * 2 if aliased else []) + [
            pl.BlockSpec((tm, D_MODEL), lambda i: (rb0 + i, 0)),
            pl.BlockSpec((D_MODEL, N_EXPERTS), lambda i: (0, 0)),
            pl.BlockSpec((1, N_EXPERTS), lambda i: (0, 0))],
        out_specs=[pl.BlockSpec((tm, TOP_K), lambda i: (rb0 + i, 0))] * 2,
        out_shape=[jax.ShapeDtypeStruct((n, TOP_K), I32), jax.ShapeDtypeStruct((n, TOP_K), F32)],
        input_output_aliases={0: 0, 1: 1} if aliased else {},
        compiler_params=_cparams(("parallel",)),
    )(*(list(bufs) if aliased else []), x, w_r, b_r)


def _moe_plan(top_i):
    flat = top_i.reshape(N_ASSIGN)
    onehot = (flat[:, None] == jnp.arange(N_EXPERTS, dtype=I32)[None, :]).astype(I32)
    csum = jnp.cumsum(onehot, axis=0)
    rank = jnp.sum(csum * onehot, axis=1) - 1
    counts = csum[-1]
    tiles = (counts + MOE_TM - 1) // MOE_TM
    tile_end = jnp.cumsum(tiles)
    tile_start = tile_end - tiles
    dest = tile_start[flat] * MOE_TM + rank
    row_token = jnp.zeros((MOE_P,), I32).at[dest].set(jnp.arange(N_ASSIGN, dtype=I32) // TOP_K)
    groups = (tiles + MOE_RT - 1) // MOE_RT
    group_end = jnp.cumsum(groups)
    n_groups = group_end[-1]
    gid = jnp.arange(MOE_G, dtype=I32)
    g_eff = jnp.minimum(gid, n_groups - 1)
    g_exp = jnp.minimum(jnp.searchsorted(group_end, g_eff, side="right").astype(I32), N_EXPERTS - 1)
    local = g_eff - (group_end - groups)[g_exp]
    g_tile0 = tile_start[g_exp] + local * MOE_RT
    g_nt = jnp.where(gid < n_groups, jnp.clip(tiles[g_exp] - local * MOE_RT, 0, MOE_RT), 0)
    return (dest.astype(I32), row_token, g_exp, g_tile0.astype(I32), g_nt.astype(I32),
            n_groups.reshape(1).astype(I32))


def _moe_expert_kernel(gexp_ref, gtile0_ref, gnt_ref, ng_ref, tok_ref,
                       x_hbm, wg_ref, wu_ref, bg_ref, bu_ref, wd_ref, bd_ref, y_hbm,
                       xg_ref, acc_ref, gsem, ssem):
    g = pl.program_id(0)
    c = pl.program_id(1)
    last = pl.num_programs(1) - 1
    nt = gnt_ref[g]
    row0 = gtile0_ref[g] * MOE_TM
    slot = g & 1

    def gather(grp, slot, start):
        first = gtile0_ref[grp] * MOE_TM

        def rows(rb, _):
            for u in range(DMA_UNROLL):
                r = rb * DMA_UNROLL + u
                cp = pltpu.make_async_copy(x_hbm.at[pl.ds(tok_ref[first + r], 1)],
                                           xg_ref.at[slot, pl.ds(r, 1)], gsem.at[slot])
                cp.start() if start else cp.wait()
            return 0

        lax.fori_loop(0, gnt_ref[grp] * (MOE_TM // DMA_UNROLL), rows, 0)

    @pl.when((c == 0) & (g == 0))
    def _first_rows():
        gather(g, slot, True)

    @pl.when((c == 0) & (nt > 0))
    def _rows():
        gather(g, slot, False)

        @pl.when(g + 1 < ng_ref[0])
        def _prefetch():
            gather(g + 1, 1 - slot, True)

    def tile_store(t):
        r0 = pl.multiple_of(t * MOE_TM, MOE_TM)
        return pltpu.make_async_copy(acc_ref.at[pl.ds(r0, MOE_TM)],
                                     y_hbm.at[pl.ds(pl.multiple_of(row0 + r0, MOE_TM), MOE_TM)], ssem)

    @pl.when((nt > 0) & (c == 0))
    def _start_from_bias():
        def fill(t, _):
            r0 = pl.multiple_of(t * MOE_TM, MOE_TM)
            acc_ref[pl.ds(r0, MOE_TM), :] = jnp.broadcast_to(bd_ref[0, 0], (MOE_TM, D_MODEL))
            return 0

        lax.fori_loop(0, nt, fill, 0)

    @pl.when(nt > 0)
    def _compute():
        wg = wg_ref[0, 0].astype(BF16)
        wu = wu_ref[0, 0].astype(BF16)
        wd = wd_ref[0, 0].astype(BF16)

        def tile(t):
            r0 = pl.multiple_of(t * MOE_TM, MOE_TM)
            xt = xg_ref[slot, pl.ds(r0, MOE_TM), :].astype(BF16)
            hg = jnp.dot(xt, wg, preferred_element_type=F32) + bg_ref[0, 0]
            hu = jnp.dot(xt, wu, preferred_element_type=F32) + bu_ref[0, 0]
            gate = jnp.minimum(hg, SWIGLU_LIMIT)
            lin = jnp.clip(hu, -SWIGLU_LIMIT, SWIGLU_LIMIT)
            act = gate * jax.nn.sigmoid(SWIGLU_ALPHA * gate) * (lin + 1.0)
            acc_ref[pl.ds(r0, MOE_TM), :] += jnp.dot(act.astype(BF16), wd, preferred_element_type=F32)

        def pair(tp, _):
            tile(2 * tp)
            tile(2 * tp + 1)
            return 0

        lax.fori_loop(0, lax.shift_right_logical(nt, 1), pair, 0)

        @pl.when((nt & 1) == 1)
        def _odd_tile():
            tile(nt - 1)

    @pl.when((nt > 0) & (c == last))
    def _store():
        lax.fori_loop(0, nt, lambda t, _: (tile_store(t).start(), 0)[1], 0)
        lax.fori_loop(0, nt, lambda t, _: (tile_store(t).wait(), 0)[1], 0)


def _moe_experts(x, plan, layer, w_gu, b_gu, w_d, b_d):
    _, row_token, g_exp, g_tile0, g_nt, n_groups = plan
    n_chunks = D_FF // MOE_F
    depth = w_gu.shape[0]

    def chunk(g, c, ng):
        return jnp.where(g < ng[0], c, n_chunks - 1)

    grid_spec = pltpu.PrefetchScalarGridSpec(
        num_scalar_prefetch=5,
        grid=(MOE_G, n_chunks),
        in_specs=[
            pl.BlockSpec(memory_space=pl.ANY),
            pl.BlockSpec((1, 1, D_MODEL, MOE_F),
                         lambda g, c, ge, gt, gn, ng, tok: (layer, ge[g], 0, chunk(g, c, ng))),
            pl.BlockSpec((1, 1, D_MODEL, MOE_F),
                         lambda g, c, ge, gt, gn, ng, tok: (layer, ge[g], 0, n_chunks + chunk(g, c, ng))),
            pl.BlockSpec((1, 1, 1, MOE_F), lambda g, c, ge, gt, gn, ng, tok: (layer, ge[g], 0, chunk(g, c, ng))),
            pl.BlockSpec((1, 1, 1, MOE_F),
                         lambda g, c, ge, gt, gn, ng, tok: (layer, ge[g], 0, n_chunks + chunk(g, c, ng))),
            pl.BlockSpec((1, 1, MOE_F, D_MODEL),
                         lambda g, c, ge, gt, gn, ng, tok: (layer, ge[g], chunk(g, c, ng), 0)),
            pl.BlockSpec((1, 1, 1, D_MODEL), lambda g, c, ge, gt, gn, ng, tok: (layer, ge[g], 0, 0)),
        ],
        out_specs=pl.BlockSpec(memory_space=pl.ANY),
        scratch_shapes=[pltpu.VMEM((2, MOE_RT * MOE_TM, D_MODEL), F32),
                        pltpu.VMEM((MOE_RT * MOE_TM, D_MODEL), F32),
                        pltpu.SemaphoreType.DMA((2,)), pltpu.SemaphoreType.DMA(())],
    )
    b_gu4 = b_gu.reshape(depth, N_EXPERTS, 1, 2 * D_FF)
    return pl.pallas_call(
        _moe_expert_kernel,
        grid_spec=grid_spec,
        out_shape=jax.ShapeDtypeStruct((MOE_P, D_MODEL), F32),
        compiler_params=_cparams(("arbitrary", "arbitrary")),
    )(g_exp, g_tile0, g_nt, n_groups, row_token, x, w_gu, w_gu, b_gu4, b_gu4,
      w_d, b_d.reshape(depth, N_EXPERTS, 1, D_MODEL))


def _moe_combine_kernel(dest_ref, *refs, tm, row0):
    y_hbm, gk_ref, x_ref, g_ref, b_ref, o_ref, ybuf_ref, sem = refs[-8:]
    i = pl.program_id(0)
    slot = i & 1

    def rows(step, slot, start):
        base = (row0 + step * tm) * TOP_K

        def row(a, _):
            r, k = lax.shift_right_logical(a, TOP_K_SHIFT), a & (TOP_K - 1)
            cp = pltpu.make_async_copy(y_hbm.at[pl.ds(dest_ref[base + a], 1)],
                                       ybuf_ref.at[slot, k, pl.ds(r, 1)], sem.at[slot])
            cp.start() if start else cp.wait()
            return 0

        lax.fori_loop(0, tm * TOP_K, row, 0, unroll=DMA_UNROLL)

    @pl.when(i == 0)
    def _first():
        rows(i, slot, True)

    rows(i, slot, False)

    @pl.when(i + 1 < pl.num_programs(0))
    def _prefetch():
        rows(i + 1, 1 - slot, True)

    gk = gk_ref[...]
    out = gk[:, 0:1] * ybuf_ref[slot, 0]
    for k in range(1, TOP_K):
        out = out + gk[:, k:k + 1] * ybuf_ref[slot, k]
    _post_norm_store(x_ref[...], out, g_ref, b_ref, o_ref)


def _moe_combine_postnorm(y_sorted, dest, gk, x, row0, nrows, tm, g, b, buf=None):
    rb0 = row0 // tm
    out_spec, out_shape, alias_specs, alias_args = _row_out(tm, rb0, x.shape[0], D_MODEL, buf)
    kern = functools.partial(_moe_combine_kernel, tm=tm, row0=row0)
    grid_spec = pltpu.PrefetchScalarGridSpec(
        num_scalar_prefetch=1,
        grid=(nrows // tm,),
        in_specs=alias_specs + [
            pl.BlockSpec(memory_space=pl.ANY),
            pl.BlockSpec((tm, TOP_K), lambda i, d: (rb0 + i, 0)),
            pl.BlockSpec((tm, D_MODEL), lambda i, d: (rb0 + i, 0)),
            pl.BlockSpec((1, D_MODEL), lambda i, d: (0, 0)),
            pl.BlockSpec((1, D_MODEL), lambda i, d: (0, 0))],
        out_specs=out_spec,
        scratch_shapes=[pltpu.VMEM((2, TOP_K, tm, D_MODEL), F32), pltpu.SemaphoreType.DMA((2,))],
    )
    return pl.pallas_call(
        kern,
        grid_spec=grid_spec,
        out_shape=out_shape,
        input_output_aliases={1: 0} if buf is not None else {},
        compiler_params=_cparams(("arbitrary",)),
    )(dest, *alias_args, y_sorted, gk, x, g, b)


def _moe_postnorm(x, layer, w_r, b_r, w_gu, b_gu, w_d, b_d, g, b):
    routed = _router(x, 0, N_PROMPT, TM_PROMPT, w_r, b_r)
    top_i, gk = _router(x, N_PROMPT, N_SAMPLE, TM_SAMPLE, w_r, b_r, bufs=routed)
    plan = _moe_plan(top_i)
    y_sorted = _moe_experts(x, plan, layer, w_gu, b_gu, w_d, b_d)
    buf = _moe_combine_postnorm(y_sorted, plan[0], gk, x, 0, N_PROMPT, TM_PROMPT, g, b)
    return _moe_combine_postnorm(y_sorted, plan[0], gk, x, N_PROMPT, N_SAMPLE, TM_SAMPLE, g, b, buf=buf)


def _attn_layer(x, layer, cache_k, cache_v, cache_kidx, page_table, w_in, kn_g, kn_b, w_out, g, b):
    w_in_bf = w_in.astype(BF16)
    w_out_bf = w_out.astype(BF16)
    w_kiwi = jnp.pad(w_in_bf[:, OFF_KI:], ((0, 0), (0, 2 * LANES - (IN_W - OFF_KI))))
    kn_g = kn_g.reshape(1, IDX_DIM)
    kn_b = kn_b.reshape(1, IDX_DIM)

    def project(row0, nrows, tm, q_dtype):
        tm_mm = min(TM_MM, nrows)
        (q,) = _mm(x, row0, nrows, tm_mm, w_in_bf, 0, ATTN_W, (q_dtype,))
        k, kbf = _mm(x, row0, nrows, tm_mm, w_in_bf, OFF_K, ATTN_W, (F32, BF16))
        v, vbf = _mm(x, row0, nrows, tm_mm, w_in_bf, OFF_V, ATTN_W, (F32, BF16))
        (qi,) = _mm(x, row0, nrows, tm_mm, w_in_bf, OFF_QI, IDX_W, (BF16,))
        ki, wi = _kiwi(x, row0, nrows, tm, w_kiwi, kn_g, kn_b)
        return q, k, kbf, v, vbf, qi, ki, wi

    q, k_p, kbf, v_p, vbf, qi, ki_p, wi = project(0, N_PROMPT, TM_PROMPT, BF16)
    s3 = lambda a: a.reshape(BATCH, SEQ, a.shape[-1])
    o_p = _dsa_prompt(s3(q), s3(qi), s3(wi), s3(ki_p), s3(kbf), s3(vbf)).reshape(N_PROMPT, ATTN_W)

    q, k_s, _, v_s, _, qi, ki_s, wi = project(N_PROMPT, N_SAMPLE, TM_SAMPLE, F32)
    ki_new_pad = jnp.pad(ki_s.reshape(DEC_BATCH, DEC_SEQ, IDX_DIM), ((0, 0), (0, PAGE_SIZE - DEC_SEQ), (0, 0)))
    idx, cnt, n_sel = _dsa_sample_select(page_table, layer,
                                         qi.reshape(DEC_BATCH, DEC_SEQ * N_IDX_HEADS, IDX_DIM),
                                         wi.reshape(DEC_BATCH, DEC_SEQ * N_IDX_HEADS, 1),
                                         cache_kidx, ki_new_pad)
    heads = lambda a: a.reshape(N_SAMPLE, N_HEADS, HEAD_DIM)
    o_s = _dsa_sample_attend(idx, cnt, n_sel, page_table, layer, heads(q), cache_k, cache_v,
                             heads(k_s), heads(v_s)).reshape(N_SAMPLE, ATTN_W)

    buf = _outproj_postnorm(o_p, w_out_bf, x, 0, TM_PROMPT, g, b)
    x1 = _outproj_postnorm(o_s, w_out_bf, x, N_PROMPT, TM_SAMPLE, g, b, buf=buf)
    return x1, (k_p, v_p, ki_p, k_s, v_s, ki_s)


def _pool_layer(x, state, w_pool, scale, g, b):
    w_bf = w_pool.astype(BF16)
    scale = scale.reshape(1, D_MODEL)
    tiles = SEQ // TM_PROMPT
    hb = TM_PROMPT // HALO
    buf = _pool_postnorm(x, 0, N_PROMPT, TM_PROMPT, tiles, x,
                         lambda i: (jnp.maximum(i * hb - 1, 0), 0), 0, True, w_bf, scale, g, b)
    hist = jnp.pad(state, ((0, 0), (HALO - POOL_HIST, 0), (0, 0)))
    x1 = _pool_postnorm(x, N_PROMPT, N_SAMPLE, DEC_SEQ, 1, hist,
                        lambda i: (i, 0, 0), PAST_LEN, False, w_bf, scale, g, b, buf=buf)
    xp = x[:N_PROMPT].reshape(BATCH, SEQ, D_MODEL)
    xs = x[N_PROMPT:].reshape(DEC_BATCH, DEC_SEQ, D_MODEL)
    new_state_p = xp[:, SEQ - POOL_HIST:]
    new_state_s = jnp.concatenate([state, xs], axis=1)[:, -POOL_HIST:]
    return x1, (new_state_p, new_state_s)


def kernel(x_prompt, x_sample, cache_k, cache_v, cache_kidx, state_pool, page_table, w_attn_in, idx_knorm_g, idx_knorm_b, w_attn_out, w_pool, pool_scale, ln_g, ln_b, w_router, b_router, w_gate_up, b_gate_up, w_down, b_down):
    x = jnp.concatenate([x_prompt.reshape(N_PROMPT, D_MODEL), x_sample.reshape(N_SAMPLE, D_MODEL)], axis=0)
    attn_rows, pool_rows = [], []
    for i in range(DEPTH):
        j = i // N_MIXERS
        g0, b0 = ln_g[i, 0].reshape(1, D_MODEL), ln_b[i, 0].reshape(1, D_MODEL)
        g1, b1 = ln_g[i, 1].reshape(1, D_MODEL), ln_b[i, 1].reshape(1, D_MODEL)
        if i % N_MIXERS == 0:
            x, rows = _attn_layer(x, j, cache_k, cache_v, cache_kidx, page_table,
                                  w_attn_in[j], idx_knorm_g[j], idx_knorm_b[j], w_attn_out[j], g0, b0)
            attn_rows.append(rows)
        else:
            x, rows = _pool_layer(x, state_pool[j], w_pool[j], pool_scale[j], g0, b0)
            pool_rows.append(rows)
        x = _moe_postnorm(x, i, w_router[i], b_router[i].reshape(1, N_EXPERTS), w_gate_up, b_gate_up,
                          w_down, b_down, g1, b1)

    def stack(rows, idx, lead, tail):
        return jnp.stack([r[idx].reshape(lead + tail) for r in rows])

    hd = (N_HEADS, HEAD_DIM)
    return (x[:N_PROMPT].reshape(BATCH, SEQ, D_MODEL),
            x[N_PROMPT:].reshape(DEC_BATCH, DEC_SEQ, D_MODEL),
            stack(attn_rows, 0, (BATCH, SEQ), hd), stack(attn_rows, 1, (BATCH, SEQ), hd),
            stack(attn_rows, 2, (BATCH, SEQ), (IDX_DIM,)),
            jnp.stack([r[0] for r in pool_rows]),
            stack(attn_rows, 3, (DEC_BATCH, DEC_SEQ), hd), stack(attn_rows, 4, (DEC_BATCH, DEC_SEQ), hd),
            stack(attn_rows, 5, (DEC_BATCH, DEC_SEQ), (IDX_DIM,)),
            jnp.stack([r[1] for r in pool_rows]))
```

```python
import functools

import jax
import jax.numpy as jnp
from jax import lax
from jax.experimental import pallas as pl
from jax.experimental.pallas import tpu as pltpu

D_MODEL = 2048
BATCH = 4
SEQ = 2048
DEPTH = 4
DEC_BATCH = 8
DEC_SEQ = 8
PAST_LEN = 16384
PAGE_SIZE = 128

N_MIXERS = 2
N_HEADS = 16
HEAD_DIM = 128
ATTN_W = N_HEADS * HEAD_DIM
N_IDX_HEADS = 16
IDX_DIM = 128
IDX_W = N_IDX_HEADS * IDX_DIM
TOPK_MAX = 256
ATTN_SCALE = HEAD_DIM ** -0.5
IDX_SCALE = IDX_DIM ** -0.5
OFF_K = ATTN_W
OFF_V = 2 * ATTN_W
OFF_QI = 3 * ATTN_W
OFF_KI = OFF_QI + IDX_W
OFF_WI = OFF_KI + IDX_DIM
IN_W = OFF_WI + N_IDX_HEADS
POOL_WINDOWS = (2, 4, 8, 16)
N_POOL_GROUPS = len(POOL_WINDOWS)
POOL_GROUP = D_MODEL // N_POOL_GROUPS
POOL_HIST = max(POOL_WINDOWS) - 1
N_EXPERTS = 32
TOP_K = 4
D_FF = D_MODEL
SWIGLU_LIMIT = 7.0
SWIGLU_ALPHA = 1.702
LN_EPS = 1e-5
DEEPNORM_ALPHA = (2 * DEPTH) ** 0.25

N_PROMPT = BATCH * SEQ
N_SAMPLE = DEC_BATCH * DEC_SEQ
N_TOK = N_PROMPT + N_SAMPLE
N_PAGES = PAST_LEN // PAGE_SIZE

LANES = 128
HALO = 16
VMEM_LIMIT = 56 * 1024 * 1024

TM_PROMPT = min(256, N_PROMPT)
TM_SAMPLE = N_SAMPLE
TM_MM = min(1024, N_PROMPT)
TN_MM = min(512, ATTN_W)
TQ = min(256, SEQ)
SEL_G = min(8, N_PAGES)
MOE_TM = min(224, N_PROMPT)
MOE_RT = 5
MOE_F = min(256, D_FF)
DMA_UNROLL = 8
TOP_K_SHIFT = TOP_K.bit_length() - 1
PAGE_SHIFT = PAGE_SIZE.bit_length() - 1
N_ASSIGN = N_TOK * TOP_K
MOE_P = (N_ASSIGN // MOE_TM + N_EXPERTS) * MOE_TM
MOE_G = (N_ASSIGN // MOE_TM + N_EXPERTS) // MOE_RT + N_EXPERTS + 1

F32 = jnp.float32
BF16 = jnp.bfloat16
I32 = jnp.int32
NEG_INF = float("-inf")
INT_MIN = -2 ** 31

assert 1 << TOP_K_SHIFT == TOP_K and 1 << PAGE_SHIFT == PAGE_SIZE and N_PAGES % SEL_G == 0
assert MOE_TM % DMA_UNROLL == 0


def _cparams(sem):
    return pltpu.CompilerParams(dimension_semantics=sem, vmem_limit_bytes=VMEM_LIMIT)


def _layer_norm(x, g, b):
    mu = jnp.mean(x, axis=-1, keepdims=True)
    xc = x - mu
    var = jnp.mean(xc * xc, axis=-1, keepdims=True)
    return xc * lax.rsqrt(var + LN_EPS) * g + b


def _post_norm_store(x, h, g_ref, b_ref, o_ref):
    o_ref[...] = _layer_norm(DEEPNORM_ALPHA * x + h, g_ref[...], b_ref[...])


def _mm_kernel(x_ref, w_ref, *o_refs):
    acc = jnp.dot(x_ref[...].astype(BF16), w_ref[...], preferred_element_type=F32)
    for o in o_refs:
        o[...] = acc.astype(o.dtype)


def _mm(x, row0, nrows, tm, w, col0, ncols, out_dtypes):
    k = x.shape[1]
    tn = min(TN_MM, ncols)
    rb0, cb0 = row0 // tm, col0 // tn
    return pl.pallas_call(
        _mm_kernel,
        grid=(nrows // tm, ncols // tn),
        in_specs=[pl.BlockSpec((tm, k), lambda i, j: (rb0 + i, 0)),
                  pl.BlockSpec((k, tn), lambda i, j: (0, cb0 + j))],
        out_specs=[pl.BlockSpec((tm, tn), lambda i, j: (i, j)) for _ in out_dtypes],
        out_shape=[jax.ShapeDtypeStruct((nrows, ncols), dt) for dt in out_dtypes],
        compiler_params=_cparams(("parallel", "arbitrary")),
    )(x, w)


def _kiwi_kernel(x_ref, w_ref, g_ref, b_ref, ki_ref, wi_ref):
    p = jnp.dot(x_ref[...].astype(BF16), w_ref[...], preferred_element_type=F32)
    ki_ref[...] = _layer_norm(p[:, :IDX_DIM], g_ref[...], b_ref[...])
    wi_ref[...] = p[:, IDX_DIM:IDX_DIM + N_IDX_HEADS] * (N_IDX_HEADS ** -0.5)


def _kiwi(x, row0, nrows, tm, w_kiwi, g, b):
    k = x.shape[1]
    rb0 = row0 // tm
    wcols = w_kiwi.shape[1]
    return pl.pallas_call(
        _kiwi_kernel,
        grid=(nrows // tm,),
        in_specs=[pl.BlockSpec((tm, k), lambda i: (rb0 + i, 0)),
                  pl.BlockSpec((k, wcols), lambda i: (0, 0)),
                  pl.BlockSpec((1, IDX_DIM), lambda i: (0, 0)),
                  pl.BlockSpec((1, IDX_DIM), lambda i: (0, 0))],
        out_specs=[pl.BlockSpec((tm, IDX_DIM), lambda i: (i, 0)),
                   pl.BlockSpec((tm, N_IDX_HEADS), lambda i: (i, 0))],
        out_shape=[jax.ShapeDtypeStruct((nrows, IDX_DIM), F32),
                   jax.ShapeDtypeStruct((nrows, N_IDX_HEADS), F32)],
        compiler_params=_cparams(("parallel",)),
    )(x, w_kiwi, g, b)


def _count(mask):
    return jnp.sum(jnp.where(mask, 1.0, 0.0), axis=-1, keepdims=True)


def _kth_largest_key(key, k):
    lo = jnp.where(_count(key >= 0) >= k, 0, INT_MIN).astype(I32)

    def body(i, lo):
        cand = lo + jnp.left_shift(jnp.int32(1), 30 - i)
        return jnp.where(_count(key >= cand) >= k, cand, lo)

    return lax.fori_loop(0, 31, body, lo)


def _strictly_lower():
    before = lax.broadcasted_iota(I32, (LANES, LANES), 0) < lax.broadcasted_iota(I32, (LANES, LANES), 1)
    return jnp.where(before, 1.0, 0.0).astype(BF16)


def _select_bias(score, adm, k, key_ref, bias_ref):
    score = jnp.where(score == 0.0, 0.0, score)
    bits = pltpu.bitcast(score, I32)
    key = jnp.where(bits < 0, bits ^ jnp.int32(0x7FFFFFFF), bits)
    key = jnp.where(adm, key, INT_MIN)
    thr = jnp.maximum(_kth_largest_key(key, k), INT_MIN + 1)
    ge = key >= thr
    bias_ref[...] = jnp.where(ge, 0.0, NEG_INF)
    n_ge = _count(ge)

    @pl.when(jnp.max(n_ge) > k)
    def _ties():
        key_ref[...] = key
        need = k - _count(key > thr)
        tri = _strictly_lower()

        def tile(j, carry):
            off = pl.multiple_of(j * LANES, LANES)
            kj = key_ref[:, pl.ds(off, LANES)]
            eq = kj == thr
            e = jnp.where(eq, 1.0, 0.0)
            rank = jnp.dot(e.astype(BF16), tri, preferred_element_type=F32) + carry
            sel = (kj > thr) | (eq & (rank < need))
            bias_ref[:, pl.ds(off, LANES)] = jnp.where(sel, 0.0, NEG_INF)
            return carry + jnp.sum(e, axis=-1, keepdims=True)

        lax.fori_loop(0, key_ref.shape[1] // LANES, tile, jnp.zeros_like(need))


def _dsa_prompt_kernel(q_ref, qi_ref, wi_ref, ki_ref, k_ref, v_ref, o_ref, key_ref, bias_ref, *, n_sel):
    i = pl.program_id(1)
    n_blocks = pl.num_programs(1)
    s_full = ki_ref.shape[1]

    def attend(s_len):
        ki = ki_ref[0, :s_len].astype(BF16)
        wi = wi_ref[0]
        score = jnp.zeros((TQ, s_len), F32)
        for h in range(N_IDX_HEADS):
            d = lax.dot_general(qi_ref[0, :, h * IDX_DIM:(h + 1) * IDX_DIM], ki,
                                (((1,), (1,)), ((), ())), preferred_element_type=F32)
            score = score + jnp.maximum(d, 0.0) * wi[:, h:h + 1]
        score = score * IDX_SCALE
        t_pos = i * TQ + lax.broadcasted_iota(I32, (TQ, s_len), 0)
        adm = lax.broadcasted_iota(I32, (TQ, s_len), 1) <= t_pos
        bias = bias_ref.at[:, pl.ds(0, s_len)]
        _select_bias(jnp.where(adm, score, NEG_INF), adm, n_sel, key_ref.at[:, pl.ds(0, s_len)], bias)

        for h in range(N_HEADS):
            cols = slice(h * HEAD_DIM, (h + 1) * HEAD_DIM)
            logits = lax.dot_general(q_ref[0, :, cols], k_ref[0, :s_len, cols],
                                     (((1,), (1,)), ((), ())), preferred_element_type=F32)
            logits = logits * ATTN_SCALE + bias[...]
            m = jnp.max(logits, axis=-1, keepdims=True)
            p = jnp.exp(logits - m)
            denom = jnp.sum(p, axis=-1, keepdims=True)
            o = jnp.dot(p.astype(BF16), v_ref[0, :s_len, cols], preferred_element_type=F32)
            o_ref[0, :, cols] = (o / denom).astype(o_ref.dtype)

    half = s_full // 2
    if half % TQ == 0 and half >= max(n_sel, LANES):
        pl.when(2 * i < n_blocks)(lambda: attend(half))
        pl.when(2 * i >= n_blocks)(lambda: attend(s_full))
    else:
        attend(s_full)


def _dsa_prompt(q, qi, wi, ki, k, v):
    b, t = q.shape[:2]
    n_sel = min(TOPK_MAX, t // 4)
    kern = functools.partial(_dsa_prompt_kernel, n_sel=n_sel)
    once = pl.Buffered(1)
    return pl.pallas_call(
        kern,
        grid=(b, t // TQ),
        in_specs=[pl.BlockSpec((1, TQ, ATTN_W), lambda bb, i: (bb, i, 0)),
                  pl.BlockSpec((1, TQ, IDX_W), lambda bb, i: (bb, i, 0)),
                  pl.BlockSpec((1, TQ, N_IDX_HEADS), lambda bb, i: (bb, i, 0)),
                  pl.BlockSpec((1, t, IDX_DIM), lambda bb, i: (bb, 0, 0), pipeline_mode=once),
                  pl.BlockSpec((1, t, ATTN_W), lambda bb, i: (bb, 0, 0), pipeline_mode=once),
                  pl.BlockSpec((1, t, ATTN_W), lambda bb, i: (bb, 0, 0), pipeline_mode=once)],
        out_specs=pl.BlockSpec((1, TQ, ATTN_W), lambda bb, i: (bb, i, 0)),
        out_shape=jax.ShapeDtypeStruct((b, t, ATTN_W), BF16),
        scratch_shapes=[pltpu.VMEM((TQ, t), I32), pltpu.VMEM((TQ, t), F32)],
        compiler_params=_cparams(("parallel", "arbitrary")),
    )(q, qi, wi, ki, k, v)


def _dsa_sample_select_kernel(pt_ref, qi_ref, wi_ref, *refs, n_sel):
    cki_refs = refs[:SEL_G]
    kin_ref, idx_ref, cnt_ref, score_ref, key_ref, bias_ref, hit_ref = refs[SEL_G:]
    p = pl.program_id(1)
    tn = DEC_SEQ
    n_steps = N_PAGES // SEL_G
    s_len = score_ref.shape[1]

    def page_score(kpage, off):
        d = lax.dot_general(qi_ref[0], kpage.astype(BF16), (((1,), (1,)), ((), ())), preferred_element_type=F32)
        r = jnp.maximum(d, 0.0) * wi_ref[0]
        score_ref[:, pl.ds(off, PAGE_SIZE)] = jnp.sum(r.reshape(tn, N_IDX_HEADS, PAGE_SIZE), axis=1) * IDX_SCALE

    @pl.when(p < n_steps)
    def _cached_pages():
        for gi in range(SEL_G):
            page_score(cki_refs[gi][0, 0], pl.multiple_of((p * SEL_G + gi) * PAGE_SIZE, PAGE_SIZE))

    @pl.when(p == n_steps)
    def _new_keys_then_select():
        page_score(kin_ref[0], PAST_LEN)
        q_pos = PAST_LEN + lax.broadcasted_iota(I32, (tn, s_len), 0)
        adm = lax.broadcasted_iota(I32, (tn, s_len), 1) <= q_pos
        _select_bias(jnp.where(adm, score_ref[...], NEG_INF), adm, n_sel, key_ref, bias_ref)

        hit_ref[...] = jnp.zeros_like(hit_ref)
        tri = _strictly_lower()
        slot = lax.broadcasted_iota(I32, (n_sel, LANES), 0).astype(F32)
        lane = lax.broadcasted_iota(I32, (1, LANES), 1)

        def tile(t, carry):
            off = pl.multiple_of(t * LANES, LANES)
            e = jnp.where(bias_ref[:, pl.ds(off, LANES)] == 0.0, 1.0, 0.0)
            rank = jnp.dot(e.astype(BF16), tri, preferred_element_type=F32) + carry
            pos = (lane + off).astype(F32)
            for q in range(tn):
                hit = (rank[q:q + 1, :] == slot) & (e[q:q + 1, :] > 0.0)
                hit_ref[q] += jnp.where(hit, pos, 0.0)
            return carry + jnp.sum(e, axis=-1, keepdims=True)

        cnt = lax.fori_loop(0, s_len // LANES, tile, jnp.zeros((tn, 1), F32))
        for q in range(tn):
            idx_ref[0, q * n_sel:(q + 1) * n_sel, :] = jnp.sum(hit_ref[q], axis=-1, keepdims=True).astype(I32)
        cnt_ref[0] = cnt.astype(I32)


def _dsa_sample_select(page_table, layer, qi_flat, wi_flat, cki, ki_new_pad):
    s_len = PAST_LEN + PAGE_SIZE
    n_sel = min(TOPK_MAX, (PAST_LEN + DEC_SEQ) // 4)
    rows = DEC_SEQ * N_IDX_HEADS
    n_steps = N_PAGES // SEL_G
    kern = functools.partial(_dsa_sample_select_kernel, n_sel=n_sel)

    def page_spec(gi):
        def index(b, p, pt):
            return (layer, pt[b * N_PAGES + jnp.minimum(p, n_steps - 1) * SEL_G + gi], 0, 0)
        return pl.BlockSpec((1, 1, PAGE_SIZE, IDX_DIM), index)

    grid_spec = pltpu.PrefetchScalarGridSpec(
        num_scalar_prefetch=1,
        grid=(DEC_BATCH, n_steps + 1),
        in_specs=[pl.BlockSpec((1, rows, IDX_DIM), lambda b, p, pt: (b, 0, 0)),
                  pl.BlockSpec((1, rows, 1), lambda b, p, pt: (b, 0, 0))]
                 + [page_spec(gi) for gi in range(SEL_G)]
                 + [pl.BlockSpec((1, PAGE_SIZE, IDX_DIM), lambda b, p, pt: (b, 0, 0))],
        out_specs=[pl.BlockSpec((1, DEC_SEQ * n_sel, 1), lambda b, p, pt: (b, 0, 0)),
                   pl.BlockSpec((1, DEC_SEQ, 1), lambda b, p, pt: (b, 0, 0))],
        scratch_shapes=[pltpu.VMEM((DEC_SEQ, s_len), F32), pltpu.VMEM((DEC_SEQ, s_len), I32),
                        pltpu.VMEM((DEC_SEQ, s_len), F32), pltpu.VMEM((DEC_SEQ, n_sel, LANES), F32)],
    )
    idx, cnt = pl.pallas_call(
        kern,
        grid_spec=grid_spec,
        out_shape=[jax.ShapeDtypeStruct((DEC_BATCH, DEC_SEQ * n_sel, 1), I32),
                   jax.ShapeDtypeStruct((DEC_BATCH, DEC_SEQ, 1), I32)],
        compiler_params=_cparams(("parallel", "arbitrary")),
    )(page_table.reshape(-1), qi_flat, wi_flat, *([cki] * SEL_G), ki_new_pad)
    return idx.reshape(-1), cnt.reshape(-1), n_sel


def _dsa_sample_attend_kernel(idx_ref, cnt_ref, pt_ref, q_ref, ck_hbm, cv_hbm, kn_hbm, vn_hbm, o_ref,
                              kbuf, vbuf, ksem, vsem, *, layer, n_sel):
    i = pl.program_id(0)
    slot = i & 1

    def fetch(step, slot):
        b = lax.div(step, DEC_SEQ)

        def row(j, _):
            s = idx_ref[step * n_sel + j]

            @pl.when(s < PAST_LEN)
            def _cached():
                page = pt_ref[b * N_PAGES + lax.shift_right_logical(s, PAGE_SHIFT)]
                r = s & (PAGE_SIZE - 1)
                pltpu.make_async_copy(ck_hbm.at[layer, page, r], kbuf.at[slot, j], ksem.at[slot]).start()
                pltpu.make_async_copy(cv_hbm.at[layer, page, r], vbuf.at[slot, j], vsem.at[slot]).start()

            @pl.when(s >= PAST_LEN)
            def _new():
                n = b * DEC_SEQ + (s - PAST_LEN)
                pltpu.make_async_copy(kn_hbm.at[n], kbuf.at[slot, j], ksem.at[slot]).start()
                pltpu.make_async_copy(vn_hbm.at[n], vbuf.at[slot, j], vsem.at[slot]).start()

            return 0

        lax.fori_loop(0, n_sel, row, 0)

    def wait(slot):
        def row(j, _):
            pltpu.make_async_copy(kn_hbm.at[0], kbuf.at[slot, j], ksem.at[slot]).wait()
            pltpu.make_async_copy(vn_hbm.at[0], vbuf.at[slot, j], vsem.at[slot]).wait()
            return 0

        lax.fori_loop(0, n_sel, row, 0, unroll=DMA_UNROLL)

    @pl.when(i == 0)
    def _first():
        fetch(i, slot)

    wait(slot)

    @pl.when(i + 1 < pl.num_programs(0))
    def _prefetch():
        fetch(i + 1, 1 - slot)

    k = kbuf[slot]
    logits = jnp.sum(k * q_ref[...], axis=-1, keepdims=True) * ATTN_SCALE
    valid = lax.broadcasted_iota(I32, logits.shape, 0) < cnt_ref[i]
    logits = jnp.where(valid, logits, NEG_INF)
    m = jnp.max(logits, axis=0, keepdims=True)
    p = jnp.exp(logits - m)
    denom = jnp.sum(p, axis=0, keepdims=True)
    o_ref[...] = jnp.sum(p * vbuf[slot], axis=0, keepdims=True) / denom


def _dsa_sample_attend(idx, cnt, n_sel, page_table, layer, q, ck, cv, k_new, v_new):
    n = q.shape[0]
    grid_spec = pltpu.PrefetchScalarGridSpec(
        num_scalar_prefetch=3,
        grid=(n,),
        in_specs=[pl.BlockSpec((1, N_HEADS, HEAD_DIM), lambda i, *_: (i, 0, 0))]
                 + [pl.BlockSpec(memory_space=pl.ANY)] * 4,
        out_specs=pl.BlockSpec((1, N_HEADS, HEAD_DIM), lambda i, *_: (i, 0, 0)),
        scratch_shapes=[pltpu.VMEM((2, n_sel, N_HEADS, HEAD_DIM), F32),
                        pltpu.VMEM((2, n_sel, N_HEADS, HEAD_DIM), F32),
                        pltpu.SemaphoreType.DMA((2,)), pltpu.SemaphoreType.DMA((2,))],
    )
    return pl.pallas_call(
        functools.partial(_dsa_sample_attend_kernel, layer=layer, n_sel=n_sel),
        grid_spec=grid_spec,
        out_shape=jax.ShapeDtypeStruct((n, N_HEADS, HEAD_DIM), F32),
        compiler_params=_cparams(("arbitrary",)),
    )(idx, cnt, page_table.reshape(-1), q, ck, cv, k_new, v_new)


def _row_out(tm, rb0, n_tok_rows, width, buf):
    spec = pl.BlockSpec((tm, width), lambda i, *_: (rb0 + i, 0))
    shape = jax.ShapeDtypeStruct((n_tok_rows, width), F32)
    alias_specs = [pl.BlockSpec(memory_space=pl.ANY)] if buf is not None else []
    alias_args = [buf] if buf is not None else []
    return spec, shape, alias_specs, alias_args


def _outproj_kernel(*refs):
    o_ref, w_ref, x_ref, g_ref, b_ref, y_ref = refs[-6:]
    h = jnp.dot(o_ref[...].astype(BF16), w_ref[...], preferred_element_type=F32)
    _post_norm_store(x_ref[...], h, g_ref, b_ref, y_ref)


def _outproj_postnorm(o, w_out, x, row0, tm, g, b, buf=None):
    nrows, kdim = o.shape
    d = w_out.shape[1]
    rb0 = row0 // tm
    out_spec, out_shape, alias_specs, alias_args = _row_out(tm, rb0, x.shape[0], d, buf)
    return pl.pallas_call(
        _outproj_kernel,
        grid=(nrows // tm,),
        in_specs=alias_specs + [
            pl.BlockSpec((tm, kdim), lambda i: (i, 0)),
            pl.BlockSpec((kdim, d), lambda i: (0, 0)),
            pl.BlockSpec((tm, d), lambda i: (rb0 + i, 0)),
            pl.BlockSpec((1, d), lambda i: (0, 0)),
            pl.BlockSpec((1, d), lambda i: (0, 0))],
        out_specs=out_spec,
        out_shape=out_shape,
        input_output_aliases={0: 0} if buf is not None else {},
        compiler_params=_cparams(("parallel",)),
    )(*alias_args, o, w_out, x, g, b)


def _pool_kernel(*refs, tm, tiles_per_seq, start_pos, zero_first):
    x_ref, halo_ref, w_ref, sc_ref, g_ref, b_ref, y_ref, xe_ref, h_ref = refs[-9:]
    i = pl.program_id(0) % tiles_per_seq
    halo = halo_ref[...].reshape(HALO, D_MODEL)
    if zero_first:
        halo = jnp.where(i == 0, 0.0, halo)
    x = x_ref[...]
    xe_ref[0:HALO, :] = halo
    xe_ref[HALO:HALO + tm, :] = x
    pos = start_pos + i * tm + lax.broadcasted_iota(I32, (tm, 1), 0)
    for gi, w in enumerate(POOL_WINDOWS):
        cols = slice(gi * POOL_GROUP, (gi + 1) * POOL_GROUP)
        win = x[:, cols]
        for j in range(1, w):
            win = win + xe_ref[HALO - j:HALO - j + tm, cols]
        count = jnp.minimum(w, pos + 1).astype(F32)
        mixed = win / count - x[:, cols]
        y = jnp.dot(mixed.astype(BF16), w_ref[gi], preferred_element_type=F32)
        h_ref[:, cols] = y * sc_ref[:, cols]
    _post_norm_store(x, h_ref[...], g_ref, b_ref, y_ref)


def _pool_postnorm(x, row0, nrows, tm, tiles_per_seq, halo, halo_map, start_pos, zero_first,
                   w_pool, scale, g, b, buf=None):
    rb0 = row0 // tm
    out_spec, out_shape, alias_specs, alias_args = _row_out(tm, rb0, x.shape[0], D_MODEL, buf)
    kern = functools.partial(_pool_kernel, tm=tm, tiles_per_seq=tiles_per_seq,
                             start_pos=start_pos, zero_first=zero_first)
    halo_block = (HALO, D_MODEL) if halo.ndim == 2 else (1, HALO, D_MODEL)
    return pl.pallas_call(
        kern,
        grid=(nrows // tm,),
        in_specs=alias_specs + [
            pl.BlockSpec((tm, D_MODEL), lambda i: (rb0 + i, 0)),
            pl.BlockSpec(halo_block, halo_map),
            pl.BlockSpec((N_POOL_GROUPS, POOL_GROUP, POOL_GROUP), lambda i: (0, 0, 0)),
            pl.BlockSpec((1, D_MODEL), lambda i: (0, 0)),
            pl.BlockSpec((1, D_MODEL), lambda i: (0, 0)),
            pl.BlockSpec((1, D_MODEL), lambda i: (0, 0))],
        out_specs=out_spec,
        out_shape=out_shape,
        input_output_aliases={0: 0} if buf is not None else {},
        scratch_shapes=[pltpu.VMEM((HALO + tm, D_MODEL), F32), pltpu.VMEM((tm, D_MODEL), F32)],
        compiler_params=_cparams(("parallel",)),
    )(*alias_args, x, halo, w_pool, scale, g, b)


def _split_bf16(a):
    hi = a.astype(BF16)
    lo = (a - hi.astype(F32)).astype(BF16)
    return hi, lo


def _router_kernel(*refs):
    x_ref, w_ref, b_ref, ti_ref, gk_ref = refs[-5:]
    xh, xl = _split_bf16(x_ref[...])
    wh, wl = _split_bf16(w_ref[...])
    logits = (jnp.dot(xh, wh, preferred_element_type=F32) + jnp.dot(xh, wl, preferred_element_type=F32)
              + jnp.dot(xl, wh, preferred_element_type=F32)) + b_ref[...]
    tm = logits.shape[0]
    lane = lax.broadcasted_iota(I32, (tm, N_EXPERTS), 1).astype(F32)
    slot = lax.broadcasted_iota(I32, (tm, TOP_K), 1)
    top_v = jnp.zeros((tm, TOP_K), F32)
    top_i = jnp.zeros((tm, TOP_K), F32)
    for k in range(TOP_K):
        m = jnp.max(logits, axis=-1, keepdims=True)
        idx = jnp.min(jnp.where(logits == m, lane, float(N_EXPERTS)), axis=-1, keepdims=True)
        top_v = jnp.where(slot == k, m, top_v)
        top_i = jnp.where(slot == k, idx, top_i)
        logits = jnp.where(lane == idx, NEG_INF, logits)
    e = jnp.exp(top_v - top_v[:, 0:1])
    gk_ref[...] = e / jnp.sum(e, axis=-1, keepdims=True)
    ti_ref[...] = top_i.astype(I32)


def _router(x, row0, nrows, tm, w_r, b_r, bufs=None):
    rb0 = row0 // tm
    n = x.shape[0]
    aliased = bufs is not None
    return pl.pallas_call(
        _router_kernel,
        grid=(nrows // tm,),
        in_specs=([pl.BlockSpec(memory_space=pl.ANY)] * 2 if aliased else []) + [
            pl.BlockSpec((tm, D_MODEL), lambda i: (rb0 + i, 0)),
            pl.BlockSpec((D_MODEL, N_EXPERTS), lambda i: (0, 0)),
            pl.BlockSpec((1, N_EXPERTS), lambda i: (0, 0))],
        out_specs=[pl.BlockSpec((tm, TOP_K), lambda i: (rb0 + i, 0))] * 2,
        out_shape=[jax.ShapeDtypeStruct((n, TOP_K), I32), jax.ShapeDtypeStruct((n, TOP_K), F32)],
        input_output_aliases={0: 0, 1: 1} if aliased else {},
        compiler_params=_cparams(("parallel",)),
    )(*(list(bufs) if aliased else []), x, w_r, b_r)


def _moe_plan(top_i):
    flat = top_i.reshape(N_ASSIGN)
    onehot = (flat[:, None] == jnp.arange(N_EXPERTS, dtype=I32)[None, :]).astype(I32)
    csum = jnp.cumsum(onehot, axis=0)
    rank = jnp.sum(csum * onehot, axis=1) - 1
    counts = csum[-1]
    tiles = (counts + MOE_TM - 1) // MOE_TM
    tile_end = jnp.cumsum(tiles)
    tile_start = tile_end - tiles
    dest = tile_start[flat] * MOE_TM + rank
    row_token = jnp.zeros((MOE_P,), I32).at[dest].set(jnp.arange(N_ASSIGN, dtype=I32) // TOP_K)
    groups = (tiles + MOE_RT - 1) // MOE_RT
    group_end = jnp.cumsum(groups)
    n_groups = group_end[-1]
    gid = jnp.arange(MOE_G, dtype=I32)
    g_eff = jnp.minimum(gid, n_groups - 1)
    g_exp = jnp.minimum(jnp.searchsorted(group_end, g_eff, side="right").astype(I32), N_EXPERTS - 1)
    local = g_eff - (group_end - groups)[g_exp]
    g_tile0 = tile_start[g_exp] + local * MOE_RT
    g_nt = jnp.where(gid < n_groups, jnp.clip(tiles[g_exp] - local * MOE_RT, 0, MOE_RT), 0)
    return (dest.astype(I32), row_token, g_exp, g_tile0.astype(I32), g_nt.astype(I32),
            n_groups.reshape(1).astype(I32))


def _moe_expert_kernel(gexp_ref, gtile0_ref, gnt_ref, ng_ref, tok_ref,
                       x_hbm, wg_ref, wu_ref, bg_ref, bu_ref, wd_ref, bd_ref, y_hbm,
                       xg_ref, acc_ref, gsem, ssem):
    g = pl.program_id(0)
    c = pl.program_id(1)
    last = pl.num_programs(1) - 1
    nt = gnt_ref[g]
    row0 = gtile0_ref[g] * MOE_TM
    slot = g & 1

    def gather(grp, slot, start):
        first = gtile0_ref[grp] * MOE_TM

        def rows(rb, _):
            for u in range(DMA_UNROLL):
                r = rb * DMA_UNROLL + u
                cp = pltpu.make_async_copy(x_hbm.at[pl.ds(tok_ref[first + r], 1)],
                                           xg_ref.at[slot, pl.ds(r, 1)], gsem.at[slot])
                cp.start() if start else cp.wait()
            return 0

        lax.fori_loop(0, gnt_ref[grp] * (MOE_TM // DMA_UNROLL), rows, 0)

    @pl.when((c == 0) & (g == 0))
    def _first_rows():
        gather(g, slot, True)

    @pl.when((c == 0) & (nt > 0))
    def _rows():
        gather(g, slot, False)

        @pl.when(g + 1 < ng_ref[0])
        def _prefetch():
            gather(g + 1, 1 - slot, True)

    def tile_store(t):
        r0 = pl.multiple_of(t * MOE_TM, MOE_TM)
        return pltpu.make_async_copy(acc_ref.at[pl.ds(r0, MOE_TM)],
                                     y_hbm.at[pl.ds(pl.multiple_of(row0 + r0, MOE_TM), MOE_TM)], ssem)

    @pl.when((nt > 0) & (c == 0))
    def _start_from_bias():
        def fill(t, _):
            r0 = pl.multiple_of(t * MOE_TM, MOE_TM)
            acc_ref[pl.ds(r0, MOE_TM), :] = jnp.broadcast_to(bd_ref[0, 0], (MOE_TM, D_MODEL))
            return 0

        lax.fori_loop(0, nt, fill, 0)

    @pl.when(nt > 0)
    def _compute():
        wg = wg_ref[0, 0].astype(BF16)
        wu = wu_ref[0, 0].astype(BF16)
        wd = wd_ref[0, 0].astype(BF16)

        def tile(t):
            r0 = pl.multiple_of(t * MOE_TM, MOE_TM)
            xt = xg_ref[slot, pl.ds(r0, MOE_TM), :].astype(BF16)
            hg = jnp.dot(xt, wg, preferred_element_type=F32) + bg_ref[0, 0]
            hu = jnp.dot(xt, wu, preferred_element_type=F32) + bu_ref[0, 0]
            gate = jnp.minimum(hg, SWIGLU_LIMIT)
            lin = jnp.clip(hu, -SWIGLU_LIMIT, SWIGLU_LIMIT)
            act = gate * jax.nn.sigmoid(SWIGLU_ALPHA * gate) * (lin + 1.0)
            acc_ref[pl.ds(r0, MOE_TM), :] += jnp.dot(act.astype(BF16), wd, preferred_element_type=F32)

        def pair(tp, _):
            tile(2 * tp)
            tile(2 * tp + 1)
            return 0

        lax.fori_loop(0, lax.shift_right_logical(nt, 1), pair, 0)

        @pl.when((nt & 1) == 1)
        def _odd_tile():
            tile(nt - 1)

    @pl.when((nt > 0) & (c == last))
    def _store():
        lax.fori_loop(0, nt, lambda t, _: (tile_store(t).start(), 0)[1], 0)
        lax.fori_loop(0, nt, lambda t, _: (tile_store(t).wait(), 0)[1], 0)


def _moe_experts(x, plan, layer, w_gu, b_gu, w_d, b_d):
    _, row_token, g_exp, g_tile0, g_nt, n_groups = plan
    n_chunks = D_FF // MOE_F
    depth = w_gu.shape[0]

    def chunk(g, c, ng):
        return jnp.where(g < ng[0], c, n_chunks - 1)

    grid_spec = pltpu.PrefetchScalarGridSpec(
        num_scalar_prefetch=5,
        grid=(MOE_G, n_chunks),
        in_specs=[
            pl.BlockSpec(memory_space=pl.ANY),
            pl.BlockSpec((1, 1, D_MODEL, MOE_F),
                         lambda g, c, ge, gt, gn, ng, tok: (layer, ge[g], 0, chunk(g, c, ng))),
            pl.BlockSpec((1, 1, D_MODEL, MOE_F),
                         lambda g, c, ge, gt, gn, ng, tok: (layer, ge[g], 0, n_chunks + chunk(g, c, ng))),
            pl.BlockSpec((1, 1, 1, MOE_F), lambda g, c, ge, gt, gn, ng, tok: (layer, ge[g], 0, chunk(g, c, ng))),
            pl.BlockSpec((1, 1, 1, MOE_F),
                         lambda g, c, ge, gt, gn, ng, tok: (layer, ge[g], 0, n_chunks + chunk(g, c, ng))),
            pl.BlockSpec((1, 1, MOE_F, D_MODEL),
                         lambda g, c, ge, gt, gn, ng, tok: (layer, ge[g], chunk(g, c, ng), 0)),
            pl.BlockSpec((1, 1, 1, D_MODEL), lambda g, c, ge, gt, gn, ng, tok: (layer, ge[g], 0, 0)),
        ],
        out_specs=pl.BlockSpec(memory_space=pl.ANY),
        scratch_shapes=[pltpu.VMEM((2, MOE_RT * MOE_TM, D_MODEL), F32),
                        pltpu.VMEM((MOE_RT * MOE_TM, D_MODEL), F32),
                        pltpu.SemaphoreType.DMA((2,)), pltpu.SemaphoreType.DMA(())],
    )
    b_gu4 = b_gu.reshape(depth, N_EXPERTS, 1, 2 * D_FF)
    return pl.pallas_call(
        _moe_expert_kernel,
        grid_spec=grid_spec,
        out_shape=jax.ShapeDtypeStruct((MOE_P, D_MODEL), F32),
        compiler_params=_cparams(("arbitrary", "arbitrary")),
    )(g_exp, g_tile0, g_nt, n_groups, row_token, x, w_gu, w_gu, b_gu4, b_gu4,
      w_d, b_d.reshape(depth, N_EXPERTS, 1, D_MODEL))


def _moe_combine_kernel(dest_ref, *refs, tm, row0):
    y_hbm, gk_ref, x_ref, g_ref, b_ref, o_ref, ybuf_ref, sem = refs[-8:]
    i = pl.program_id(0)
    slot = i & 1

    def rows(step, slot, start):
        base = (row0 + step * tm) * TOP_K

        def row(a, _):
            r, k = lax.shift_right_logical(a, TOP_K_SHIFT), a & (TOP_K - 1)
            cp = pltpu.make_async_copy(y_hbm.at[pl.ds(dest_ref[base + a], 1)],
                                       ybuf_ref.at[slot, k, pl.ds(r, 1)], sem.at[slot])
            cp.start() if start else cp.wait()
            return 0

        lax.fori_loop(0, tm * TOP_K, row, 0, unroll=DMA_UNROLL)

    @pl.when(i == 0)
    def _first():
        rows(i, slot, True)

    rows(i, slot, False)

    @pl.when(i + 1 < pl.num_programs(0))
    def _prefetch():
        rows(i + 1, 1 - slot, True)

    gk = gk_ref[...]
    out = gk[:, 0:1] * ybuf_ref[slot, 0]
    for k in range(1, TOP_K):
        out = out + gk[:, k:k + 1] * ybuf_ref[slot, k]
    _post_norm_store(x_ref[...], out, g_ref, b_ref, o_ref)


def _moe_combine_postnorm(y_sorted, dest, gk, x, row0, nrows, tm, g, b, buf=None):
    rb0 = row0 // tm
    out_spec, out_shape, alias_specs, alias_args = _row_out(tm, rb0, x.shape[0], D_MODEL, buf)
    kern = functools.partial(_moe_combine_kernel, tm=tm, row0=row0)
    grid_spec = pltpu.PrefetchScalarGridSpec(
        num_scalar_prefetch=1,
        grid=(nrows // tm,),
        in_specs=alias_specs + [
            pl.BlockSpec(memory_space=pl.ANY),
            pl.BlockSpec((tm, TOP_K), lambda i, d: (rb0 + i, 0)),
            pl.BlockSpec((tm, D_MODEL), lambda i, d: (rb0 + i, 0)),
            pl.BlockSpec((1, D_MODEL), lambda i, d: (0, 0)),
            pl.BlockSpec((1, D_MODEL), lambda i, d: (0, 0))],
        out_specs=out_spec,
        scratch_shapes=[pltpu.VMEM((2, TOP_K, tm, D_MODEL), F32), pltpu.SemaphoreType.DMA((2,))],
    )
    return pl.pallas_call(
        kern,
        grid_spec=grid_spec,
        out_shape=out_shape,
        input_output_aliases={1: 0} if buf is not None else {},
        compiler_params=_cparams(("arbitrary",)),
    )(dest, *alias_args, y_sorted, gk, x, g, b)


def _moe_postnorm(x, layer, w_r, b_r, w_gu, b_gu, w_d, b_d, g, b):
    routed = _router(x, 0, N_PROMPT, TM_PROMPT, w_r, b_r)
    top_i, gk = _router(x, N_PROMPT, N_SAMPLE, TM_SAMPLE, w_r, b_r, bufs=routed)
    plan = _moe_plan(top_i)
    y_sorted = _moe_experts(x, plan, layer, w_gu, b_gu, w_d, b_d)
    buf = _moe_combine_postnorm(y_sorted, plan[0], gk, x, 0, N_PROMPT, TM_PROMPT, g, b)
    return _moe_combine_postnorm(y_sorted, plan[0], gk, x, N_PROMPT, N_SAMPLE, TM_SAMPLE, g, b, buf=buf)


def _attn_layer(x, layer, cache_k, cache_v, cache_kidx, page_table, w_in, kn_g, kn_b, w_out, g, b):
    w_in_bf = w_in.astype(BF16)
    w_out_bf = w_out.astype(BF16)
    w_kiwi = jnp.pad(w_in_bf[:, OFF_KI:], ((0, 0), (0, 2 * LANES - (IN_W - OFF_KI))))
    kn_g = kn_g.reshape(1, IDX_DIM)
    kn_b = kn_b.reshape(1, IDX_DIM)

    def project(row0, nrows, tm, q_dtype):
        tm_mm = min(TM_MM, nrows)
        (q,) = _mm(x, row0, nrows, tm_mm, w_in_bf, 0, ATTN_W, (q_dtype,))
        k, kbf = _mm(x, row0, nrows, tm_mm, w_in_bf, OFF_K, ATTN_W, (F32, BF16))
        v, vbf = _mm(x, row0, nrows, tm_mm, w_in_bf, OFF_V, ATTN_W, (F32, BF16))
        (qi,) = _mm(x, row0, nrows, tm_mm, w_in_bf, OFF_QI, IDX_W, (BF16,))
        ki, wi = _kiwi(x, row0, nrows, tm, w_kiwi, kn_g, kn_b)
        return q, k, kbf, v, vbf, qi, ki, wi

    q, k_p, kbf, v_p, vbf, qi, ki_p, wi = project(0, N_PROMPT, TM_PROMPT, BF16)
    s3 = lambda a: a.reshape(BATCH, SEQ, a.shape[-1])
    o_p = _dsa_prompt(s3(q), s3(qi), s3(wi), s3(ki_p), s3(kbf), s3(vbf)).reshape(N_PROMPT, ATTN_W)

    q, k_s, _, v_s, _, qi, ki_s, wi = project(N_PROMPT, N_SAMPLE, TM_SAMPLE, F32)
    ki_new_pad = jnp.pad(ki_s.reshape(DEC_BATCH, DEC_SEQ, IDX_DIM), ((0, 0), (0, PAGE_SIZE - DEC_SEQ), (0, 0)))
    idx, cnt, n_sel = _dsa_sample_select(page_table, layer,
                                         qi.reshape(DEC_BATCH, DEC_SEQ * N_IDX_HEADS, IDX_DIM),
                                         wi.reshape(DEC_BATCH, DEC_SEQ * N_IDX_HEADS, 1),
                                         cache_kidx, ki_new_pad)
    heads = lambda a: a.reshape(N_SAMPLE, N_HEADS, HEAD_DIM)
    o_s = _dsa_sample_attend(idx, cnt, n_sel, page_table, layer, heads(q), cache_k, cache_v,
                             heads(k_s), heads(v_s)).reshape(N_SAMPLE, ATTN_W)

    buf = _outproj_postnorm(o_p, w_out_bf, x, 0, TM_PROMPT, g, b)
    x1 = _outproj_postnorm(o_s, w_out_bf, x, N_PROMPT, TM_SAMPLE, g, b, buf=buf)
    return x1, (k_p, v_p, ki_p, k_s, v_s, ki_s)


def _pool_layer(x, state, w_pool, scale, g, b):
    w_bf = w_pool.astype(BF16)
    scale = scale.reshape(1, D_MODEL)
    tiles = SEQ // TM_PROMPT
    hb = TM_PROMPT // HALO
    buf = _pool_postnorm(x, 0, N_PROMPT, TM_PROMPT, tiles, x,
                         lambda i: (jnp.maximum(i * hb - 1, 0), 0), 0, True, w_bf, scale, g, b)
    hist = jnp.pad(state, ((0, 0), (HALO - POOL_HIST, 0), (0, 0)))
    x1 = _pool_postnorm(x, N_PROMPT, N_SAMPLE, DEC_SEQ, 1, hist,
                        lambda i: (i, 0, 0), PAST_LEN, False, w_bf, scale, g, b, buf=buf)
    xp = x[:N_PROMPT].reshape(BATCH, SEQ, D_MODEL)
    xs = x[N_PROMPT:].reshape(DEC_BATCH, DEC_SEQ, D_MODEL)
    new_state_p = xp[:, SEQ - POOL_HIST:]
    new_state_s = jnp.concatenate([state, xs], axis=1)[:, -POOL_HIST:]
    return x1, (new_state_p, new_state_s)


def kernel(x_prompt, x_sample, cache_k, cache_v, cache_kidx, state_pool, page_table, w_attn_in, idx_knorm_g, idx_knorm_b, w_attn_out, w_pool, pool_scale, ln_g, ln_b, w_router, b_router, w_gate_up, b_gate_up, w_down, b_down):
    x = jnp.concatenate([x_prompt.reshape(N_PROMPT, D_MODEL), x_sample.reshape(N_SAMPLE, D_MODEL)], axis=0)
    attn_rows, pool_rows = [], []
    for i in range(DEPTH):
        j = i // N_MIXERS
        g0, b0 = ln_g[i, 0].reshape(1, D_MODEL), ln_b[i, 0].reshape(1, D_MODEL)
        g1, b1 = ln_g[i, 1].reshape(1, D_MODEL), ln_b[i, 1].reshape(1, D_MODEL)
        if i % N_MIXERS == 0:
            x, rows = _attn_layer(x, j, cache_k, cache_v, cache_kidx, page_table,
                                  w_attn_in[j], idx_knorm_g[j], idx_knorm_b[j], w_attn_out[j], g0, b0)
            attn_rows.append(rows)
        else:
            x, rows = _pool_layer(x, state_pool[j], w_pool[j], pool_scale[j], g0, b0)
            pool_rows.append(rows)
        x = _moe_postnorm(x, i, w_router[i], b_router[i].reshape(1, N_EXPERTS), w_gate_up, b_gate_up,
                          w_down, b_down, g1, b1)

    def stack(rows, idx, lead, tail):
        return jnp.stack([r[idx].reshape(lead + tail) for r in rows])

    hd = (N_HEADS, HEAD_DIM)
    return (x[:N_PROMPT].reshape(BATCH, SEQ, D_MODEL),
            x[N_PROMPT:].reshape(DEC_BATCH, DEC_SEQ, D_MODEL),
            stack(attn_rows, 0, (BATCH, SEQ), hd), stack(attn_rows, 1, (BATCH, SEQ), hd),
            stack(attn_rows, 2, (BATCH, SEQ), (IDX_DIM,)),
            jnp.stack([r[0] for r in pool_rows]),
            stack(attn_rows, 3, (DEC_BATCH, DEC_SEQ), hd), stack(attn_rows, 4, (DEC_BATCH, DEC_SEQ), hd),
            stack(attn_rows, 5, (DEC_BATCH, DEC_SEQ), (IDX_DIM,)),
            jnp.stack([r[1] for r in pool_rows]))
```
